```python
import jax
import jax.numpy as jnp
from jax import lax
import numpy as np

D_MODEL = 2048
BATCH = 8
SEQ = 2048
DEPTH = 1
DEC_BATCH = 128
DEC_SEQ = 1
PAST_LEN = 2048
PAGE_SIZE = 128

D_CONV = D_MODEL
CONV_WIDTH = 3
N_HEADS = 16
HEAD_DIM = 128
N_KV = 4
HPG = N_HEADS // N_KV
CMP_BLOCK = 32
SEL_BLOCK = 64
SEL_TOPK = 8
WINDOW = 512
SEL_QBLOCK = 32
WIN_QBLOCK = 128
ATTN_SCALE = HEAD_DIM ** -0.5
N_EXPERTS = 64
MOE_TOPK = 6
N_GROUPS = 8
TOPK_GROUPS = 4
D_EXPERT = 1408
D_SHARED = 1408
ROUTED_SCALE = 2.5
MOE_BLOCK = 128
DN_ALPHA = (2.0 * DEPTH) ** 0.25
DN_BETA = (8.0 * DEPTH) ** -0.25
LN_EPS = 1e-5
NEG_INF = -1e30
FORCE_SCORE = 1e4
COL_SIZES = (3 * D_CONV, N_HEADS * HEAD_DIM, 6 * N_KV * HEAD_DIM, 3 * N_HEADS, 2 * D_MODEL)
N_IN_COLS = 3 * D_CONV + N_HEADS * HEAD_DIM + 6 * N_KV * HEAD_DIM + 3 * N_HEADS + 2 * D_MODEL

kernel_name = 'hybrid_shortconv_nsa_moe_deepnorm_step'


def split_cols(a, sizes):
    out, start = [], 0
    for n in sizes:
        out.append(a[..., start:start + n])
        start += n
    return out


def layer_norm(x, g, b):
    xf = x.astype(jnp.float32)
    mu = xf.mean(-1, keepdims=True)
    var = jnp.square(xf - mu).mean(-1, keepdims=True)
    return ((xf - mu) * lax.rsqrt(var + LN_EPS) * g + b).astype(x.dtype)


def masked_softmax(s, mask):
    s = jnp.where(mask, s.astype(jnp.float32), NEG_INF)
    return jax.nn.softmax(s, axis=-1) * mask


def gather_pages(pool, page_table):
    g = pool[page_table]
    return g.reshape((page_table.shape[0], page_table.shape[1] * PAGE_SIZE) + pool.shape[2:])


def short_conv(u, prefix, conv_w):
    t = u.shape[1]
    u_ext = jnp.concatenate([prefix, u], axis=1)
    y = conv_w[0] * u_ext[:, 0:t]
    for j in range(1, CONV_WIDTH):
        y = y + conv_w[j] * u_ext[:, j:j + t]
    return y, u_ext[:, -(CONV_WIDTH - 1):]


def compressed_branch(qg, kc, vc, w_phi_k, w_phi_v, past_len):
    b, tq = qg.shape[:2]
    nc = kc.shape[1] // CMP_BLOCK
    kb = kc[:, :nc * CMP_BLOCK].reshape(b, nc, CMP_BLOCK, N_KV, HEAD_DIM)
    vb = vc[:, :nc * CMP_BLOCK].reshape(b, nc, CMP_BLOCK, N_KV, HEAD_DIM)
    k_cmp = jnp.einsum('bnlgd,lde->bnge', kb, w_phi_k)
    v_cmp = jnp.einsum('bnlgd,lde->bnge', vb, w_phi_v)
    s = jnp.einsum('btghd,bngd->bghtn', qg, k_cmp) * ATTN_SCALE
    q_pos = past_len + jnp.arange(tq)
    blk_end = (jnp.arange(nc) + 1) * CMP_BLOCK - 1
    vis = blk_end[None, :] <= q_pos[:, None]
    p = masked_softmax(s, vis)
    o = jnp.einsum('bghtn,bngd->btghd', p.astype(vc.dtype), v_cmp)
    return o, p


def selection_branch(qg, ks, vs, p_cmp, past_len):
    b, tq = qg.shape[:2]
    seq_len = ks.shape[1]
    ns = max(-(-seq_len // SEL_BLOCK), SEL_TOPK)
    ratio = SEL_BLOCK // CMP_BLOCK
    nc = p_cmp.shape[-1]
    p_pad = jnp.pad(p_cmp, ((0, 0), (0, 0), (0, 0), (0, 0), (0, ns * ratio - nc)))
    imp = p_pad.reshape(p_pad.shape[:-1] + (ns, ratio)).sum(axis=(2, 5))
    q_pos = past_len + jnp.arange(tq)
    blk = jnp.arange(ns)[None, :]
    cur = (q_pos // SEL_BLOCK)[:, None]
    forced = (blk == 0) | (blk == cur) | (blk == cur - 1)
    valid = blk * SEL_BLOCK <= q_pos[:, None]
    imp = jnp.where(valid, jnp.where(forced, FORCE_SCORE, imp), -1.0)
    _, sel_idx = lax.top_k(imp, SEL_TOPK)
    pad = ns * SEL_BLOCK - seq_len
    ksb = jnp.pad(ks, ((0, 0), (0, pad), (0, 0), (0, 0))).reshape(b, ns, SEL_BLOCK, N_KV, HEAD_DIM).transpose(0, 3, 1, 2, 4)
    vsb = jnp.pad(vs, ((0, 0), (0, pad), (0, 0), (0, 0))).reshape(b, ns, SEL_BLOCK, N_KV, HEAD_DIM).transpose(0, 3, 1, 2, 4)
    qb = SEL_QBLOCK if tq % SEL_QBLOCK == 0 else tq
    b_ix = jnp.arange(b)[:, None, None, None]
    g_ix = jnp.arange(N_KV)[None, :, None, None]

    def one_block(t0):
        qi = lax.dynamic_slice_in_dim(qg, t0, qb, axis=1)
        idx = lax.dynamic_slice_in_dim(sel_idx, t0, qb, axis=2)
        kk = ksb[b_ix, g_ix, idx]
        vv = vsb[b_ix, g_ix, idx]
        k_pos = idx[..., None] * SEL_BLOCK + jnp.arange(SEL_BLOCK)
        qp = past_len + t0 + jnp.arange(qb)
        mask = (k_pos <= qp[:, None, None])[:, :, None]
        s = jnp.einsum('bqghd,bgqksd->bghqks', qi, kk) * ATTN_SCALE
        p = masked_softmax(s.reshape(s.shape[:4] + (-1,)), mask.reshape(mask.shape[:4] + (-1,)))
        p = p.reshape(s.shape).astype(vv.dtype)
        return jnp.einsum('bghqks,bgqksd->bqghd', p, vv)

    out = lax.map(one_block, jnp.arange(tq // qb) * qb)
    return jnp.moveaxis(out, 0, 1).reshape(b, tq, N_KV, HPG, HEAD_DIM)


def window_branch(qg, kw_ext, vw_ext, past_len):
    b, tq = qg.shape[:2]
    qb = WIN_QBLOCK if tq % WIN_QBLOCK == 0 else tq
    span = WINDOW + qb

    def one_block(t0):
        qi = lax.dynamic_slice_in_dim(qg, t0, qb, axis=1)
        kk = lax.dynamic_slice_in_dim(kw_ext, t0, span, axis=1)
        vv = lax.dynamic_slice_in_dim(vw_ext, t0, span, axis=1)
        qp = (past_len + t0 + jnp.arange(qb))[:, None]
        kp = (past_len - WINDOW + t0 + jnp.arange(span))[None, :]
        mask = (kp <= qp) & (kp >= qp - WINDOW) & (kp >= 0)
        s = jnp.einsum('bqghd,bkgd->bghqk', qi, kk) * ATTN_SCALE
        p = masked_softmax(s, mask).astype(vv.dtype)
        return jnp.einsum('bghqk,bkgd->bqghd', p, vv)

    out = lax.map(one_block, jnp.arange(tq // qb) * qb)
    return jnp.moveaxis(out, 0, 1).reshape(b, tq, N_KV, HPG, HEAD_DIM)


def routed_experts(x2, idx, gate, w1, w3, w2):
    n_asg = x2.shape[0] * MOE_TOPK
    flat_e = idx.reshape(-1)
    flat_w = gate.reshape(-1)
    order = jnp.argsort(flat_e)
    se = flat_e[order]
    counts = jnp.zeros((N_EXPERTS,), jnp.int32).at[flat_e].add(1)
    padded = (counts + MOE_BLOCK - 1) // MOE_BLOCK * MOE_BLOCK
    pad_end = jnp.cumsum(padded)
    pad_start = pad_end - padded
    start = jnp.cumsum(counts) - counts
    dest = pad_start[se] + jnp.arange(n_asg) - start[se]
    n_blocks = -(-n_asg // MOE_BLOCK) + N_EXPERTS
    rows = n_blocks * MOE_BLOCK
    tok = jnp.zeros((rows,), jnp.int32).at[dest].set((order // MOE_TOPK).astype(jnp.int32))
    wgt = jnp.zeros((rows,), x2.dtype).at[dest].set(flat_w[order])
    blk_e = jnp.minimum(jnp.searchsorted(pad_end, jnp.arange(n_blocks) * MOE_BLOCK, side='right'), N_EXPERTS - 1)

    def one_block(args):
        t_b, e = args
        xb = x2[t_b]
        hb = jax.nn.silu(xb @ w1[e]) * (xb @ w3[e])
        return hb @ w2[e]

    out = lax.map(one_block, (tok.reshape(n_blocks, MOE_BLOCK), blk_e))
    out = out.reshape(rows, -1) * wgt[:, None]
    return jnp.zeros_like(x2).at[tok].add(out)


def moe_ffn(h2, w_router, router_bias, w_e_gate, w_e_up, w_e_down, w_s_gate, w_s_up, w_s_down):
    t = h2.shape[0]
    s = jax.nn.sigmoid((h2 @ w_router).astype(jnp.float32))
    bsc = s + router_bias.astype(jnp.float32)
    gsc = lax.top_k(bsc.reshape(t, N_GROUPS, N_EXPERTS // N_GROUPS), 2)[0].sum(-1)
    _, gidx = lax.top_k(gsc, TOPK_GROUPS)
    gmask = (gidx[..., None] == jnp.arange(N_GROUPS)).any(axis=-2)
    emask = jnp.repeat(gmask, N_EXPERTS // N_GROUPS, axis=-1)
    _, idx = lax.top_k(jnp.where(emask, bsc, NEG_INF), MOE_TOPK)
    sel = jnp.take_along_axis(s, idx, axis=-1)
    gate = (sel / sel.sum(-1, keepdims=True) * ROUTED_SCALE).astype(h2.dtype)
    routed = routed_experts(h2, idx, gate, w_e_gate, w_e_up, w_e_down)
    shared = (jax.nn.silu(h2 @ w_s_gate) * (h2 @ w_s_up)) @ w_s_down
    return routed + shared


def decoder_layer(x, past_len, conv_prefix, win_prefix, past_cmp, past_sel, win_keep,
                  w_in, conv_w, w_phi_k, w_phi_v, w_conv_out, w_attn_out, w_o, ln1_g, ln1_b,
                  w_router, router_bias, w_e_gate, w_e_up, w_e_down, w_s_gate, w_s_up, w_s_down,
                  ln2_g, ln2_b):
    b, t, d = x.shape
    proj = jnp.einsum('btd,dn->btn', x, w_in)
    conv_cols, q, kv, nsa_g, merge_g = split_cols(proj, COL_SIZES)
    gate_b, gate_c, h = split_cols(conv_cols, (D_CONV, D_CONV, D_CONV))
    conv_y, conv_state = short_conv(gate_c * h, conv_prefix, conv_w)
    y_a = (gate_b * conv_y) @ w_conv_out
    qg = q.reshape(b, t, N_KV, HPG, HEAD_DIM)
    kv = kv.reshape(b, t, 3, 2, N_KV, HEAD_DIM)
    cmp_rows, sel_rows, win_rows = kv[:, :, 0], kv[:, :, 1], kv[:, :, 2]
    cmp_full = jnp.concatenate([past_cmp, cmp_rows], axis=1)
    sel_full = jnp.concatenate([past_sel, sel_rows], axis=1)
    win_ext = jnp.concatenate([win_prefix, win_rows], axis=1)
    o_c, p_c = compressed_branch(qg, cmp_full[:, :, 0], cmp_full[:, :, 1], w_phi_k, w_phi_v, past_len)
    o_s = selection_branch(qg, sel_full[:, :, 0], sel_full[:, :, 1], p_c, past_len)
    o_w = window_branch(qg, win_ext[:, :, 0], win_ext[:, :, 1], past_len)
    g = jax.nn.sigmoid(nsa_g.astype(jnp.float32)).astype(x.dtype).reshape(b, t, N_KV, HPG, 3)
    o = g[..., 0:1] * o_c + g[..., 1:2] * o_s + g[..., 2:3] * o_w
    y_b = o.reshape(b, t, N_HEADS * HEAD_DIM) @ w_attn_out
    g_a, g_b = split_cols(jax.nn.sigmoid(merge_g), (D_MODEL, D_MODEL))
    mix = (g_a * y_a + g_b * y_b) @ w_o
    h1 = layer_norm(DN_ALPHA * x + mix, ln1_g, ln1_b)
    ffn = moe_ffn(h1.reshape(b * t, d), w_router, router_bias, w_e_gate, w_e_up, w_e_down,
                  w_s_gate, w_s_up, w_s_down).reshape(b, t, d)
    y = layer_norm(DN_ALPHA * h1 + ffn, ln2_g, ln2_b)
    return y, cmp_rows, sel_rows, win_ext[:, -win_keep:], conv_state


def setup_inputs(seed: int = 0) -> dict:
    key = jax.random.key(seed)
    ks = jax.random.split(key, 32)
    n_pages = PAST_LEN // PAGE_SIZE
    n_used = DEC_BATCH * n_pages
    n_phys = n_used + max(1, n_used // 4)
    win_buf = min(WINDOW, PAST_LEN)
    f32 = jnp.float32

    def nrm(k, shape, scale):
        return jax.random.normal(k, shape, f32) * scale

    perm = jax.random.permutation(ks[0], n_phys)
    page_table = perm[:n_used].reshape(DEC_BATCH, n_pages).astype(jnp.int32)
    return {
        'x_prompt': nrm(ks[1], (BATCH, SEQ, D_MODEL), 1.0),
        'x_sample': nrm(ks[2], (DEC_BATCH, DEC_SEQ, D_MODEL), 1.0),
        'cache_cmp_kv': nrm(ks[3], (DEPTH, n_phys, PAGE_SIZE, 2, N_KV, HEAD_DIM), 1.0),
        'cache_sel_kv': nrm(ks[4], (DEPTH, n_phys, PAGE_SIZE, 2, N_KV, HEAD_DIM), 1.0),
        'state_win_kv': nrm(ks[5], (DEPTH, DEC_BATCH, win_buf, 2, N_KV, HEAD_DIM), 1.0),
        'state_conv': nrm(ks[6], (DEPTH, DEC_BATCH, CONV_WIDTH - 1, D_CONV), 1.0),
        'page_table': page_table,
        'w_in': nrm(ks[7], (DEPTH, D_MODEL, N_IN_COLS), D_MODEL ** -0.5),
        'conv_w': nrm(ks[8], (DEPTH, CONV_WIDTH, D_CONV), CONV_WIDTH ** -0.5),
        'w_phi_k': nrm(ks[9], (DEPTH, CMP_BLOCK, HEAD_DIM, HEAD_DIM), (CMP_BLOCK * HEAD_DIM) ** -0.5),
        'w_phi_v': nrm(ks[10], (DEPTH, CMP_BLOCK, HEAD_DIM, HEAD_DIM), (CMP_BLOCK * HEAD_DIM) ** -0.5),
        'w_conv_out': nrm(ks[11], (DEPTH, D_CONV, D_MODEL), D_CONV ** -0.5 * DN_BETA),
        'w_attn_out': nrm(ks[12], (DEPTH, N_HEADS * HEAD_DIM, D_MODEL), (N_HEADS * HEAD_DIM) ** -0.5 * DN_BETA),
        'w_o': nrm(ks[13], (DEPTH, D_MODEL, D_MODEL), D_MODEL ** -0.5 * DN_BETA),
        'ln1_g': 1.0 + nrm(ks[14], (DEPTH, D_MODEL), 0.02),
        'ln1_b': nrm(ks[15], (DEPTH, D_MODEL), 0.02),
        'w_router': nrm(ks[16], (DEPTH, D_MODEL, N_EXPERTS), D_MODEL ** -0.5),
        'router_bias': nrm(ks[17], (DEPTH, N_EXPERTS), 0.01),
        'w_e_gate': nrm(ks[18], (DEPTH, N_EXPERTS, D_MODEL, D_EXPERT), D_MODEL ** -0.5),
        'w_e_up': nrm(ks[19], (DEPTH, N_EXPERTS, D_MODEL, D_EXPERT), D_MODEL ** -0.5),
        'w_e_down': nrm(ks[20], (DEPTH, N_EXPERTS, D_EXPERT, D_MODEL), D_EXPERT ** -0.5 * DN_BETA),
        'w_s_gate': nrm(ks[21], (DEPTH, D_MODEL, D_SHARED), D_MODEL ** -0.5),
        'w_s_up': nrm(ks[22], (DEPTH, D_MODEL, D_SHARED), D_MODEL ** -0.5),
        'w_s_down': nrm(ks[23], (DEPTH, D_SHARED, D_MODEL), D_SHARED ** -0.5 * DN_BETA),
        'ln2_g': 1.0 + nrm(ks[24], (DEPTH, D_MODEL), 0.02),
        'ln2_b': nrm(ks[25], (DEPTH, D_MODEL), 0.02),
    }


def reference(x_prompt, x_sample, cache_cmp_kv, cache_sel_kv, state_win_kv, state_conv, page_table,
              w_in, conv_w, w_phi_k, w_phi_v, w_conv_out, w_attn_out, w_o, ln1_g, ln1_b,
              w_router, router_bias, w_e_gate, w_e_up, w_e_down, w_s_gate, w_s_up, w_s_down,
              ln2_g, ln2_b):
    bp, tp = x_prompt.shape[:2]
    past_len = page_table.shape[1] * PAGE_SIZE
    win_buf = state_win_kv.shape[2]
    xp, xs = x_prompt, x_sample
    pc, psl, pw, pcv, sc, ssl, sw, scv = [], [], [], [], [], [], [], []
    for l in range(DEPTH):
        lw = (w_in[l], conv_w[l], w_phi_k[l], w_phi_v[l], w_conv_out[l], w_attn_out[l], w_o[l],
              ln1_g[l], ln1_b[l], w_router[l], router_bias[l], w_e_gate[l], w_e_up[l], w_e_down[l],
              w_s_gate[l], w_s_up[l], w_s_down[l], ln2_g[l], ln2_b[l])
        empty = jnp.zeros((bp, 0, 2, N_KV, HEAD_DIM), xp.dtype)
        xp, r_c, r_s, r_w, r_cv = decoder_layer(
            xp, 0, jnp.zeros((bp, CONV_WIDTH - 1, D_CONV), xp.dtype),
            jnp.zeros((bp, WINDOW, 2, N_KV, HEAD_DIM), xp.dtype), empty, empty, min(WINDOW, tp), *lw)
        pc.append(r_c)
        psl.append(r_s)
        pw.append(r_w)
        pcv.append(r_cv)
        past_cmp = gather_pages(cache_cmp_kv[l], page_table)
        past_sel = gather_pages(cache_sel_kv[l], page_table)
        win_prefix = jnp.pad(state_win_kv[l], ((0, 0), (WINDOW - win_buf, 0), (0, 0), (0, 0), (0, 0)))
        xs, r_c, r_s, r_w, r_cv = decoder_layer(
            xs, past_len, state_conv[l], win_prefix, past_cmp, past_sel, win_buf, *lw)
        sc.append(r_c)
        ssl.append(r_s)
        sw.append(r_w)
        scv.append(r_cv)
    return (xp, xs, jnp.stack(pc), jnp.stack(psl), jnp.stack(pw), jnp.stack(pcv),
            jnp.stack(sc), jnp.stack(ssl), jnp.stack(sw), jnp.stack(scv))
```

```python
import functools

import jax
import jax.numpy as jnp
from jax import lax
from jax.experimental import pallas as pl
from jax.experimental.pallas import tpu as pltpu

F32 = jnp.float32
BF16 = jnp.bfloat16
I32 = jnp.int32

N_HEADS = 16
HEAD_DIM = 128
N_KV = 4
HPG = N_HEADS // N_KV
CMP_BLOCK = 32
SEL_BLOCK = 64
SEL_TOPK = 8
WINDOW = 512
PAGE_SIZE = 128
CONV_WIDTH = 3
N_EXPERTS = 64
MOE_TOPK = 6
N_GROUPS = 8
TOPK_GROUPS = 4
ROUTED_SCALE = 2.5
LN_EPS = 1e-5
NEG_INF = -1e30
FORCE_SCORE = 1e4
ATTN_SCALE = HEAD_DIM ** -0.5

LANES = 128
SUBLANES = 8
VMEM_LIMIT = 56 * 1024 * 1024

EXPERT_TILE = 256
ROUTE_TILE = 384
COMBINE_TILE = 64
SCATTER_TILE = 128


def _dot(a, b):
    return jnp.dot(a, b, preferred_element_type=F32)


def _dot_nt(a, b):
    return lax.dot_general(a, b, (((1,), (1,)), ((), ())), preferred_element_type=F32)


def _dot_tn(a, b):
    return lax.dot_general(a, b, (((0,), (0,)), ((), ())), preferred_element_type=F32)


def _params(sem):
    return pltpu.CompilerParams(dimension_semantics=sem, vmem_limit_bytes=VMEM_LIMIT)


def _mm_kernel(x_ref, w_ref, *o_refs):
    r = _dot(x_ref[...], w_ref[...])
    for o_ref in o_refs:
        o_ref[...] = r.astype(o_ref.dtype)


def matmul(x, w, out_dtypes, tm=1024, tn=1024):
    m, k = x.shape
    n = w.shape[1]
    tm = min(tm, m)
    tn = min(tn, n)
    assert m % tm == 0 and n % tn == 0
    outs = pl.pallas_call(
        _mm_kernel,
        grid=(n // tn, m // tm),
        in_specs=[pl.BlockSpec((tm, k), lambda j, i: (i, 0)),
                  pl.BlockSpec((k, tn), lambda j, i: (0, j))],
        out_specs=[pl.BlockSpec((tm, tn), lambda j, i: (i, j)) for _ in out_dtypes],
        out_shape=[jax.ShapeDtypeStruct((m, n), dt) for dt in out_dtypes],
        compiler_params=_params(("parallel", "parallel")),
        name="proj_mm",
    )(x, w)
    return outs


def _conv_proj_kernel(x_ref, wb_ref, wc_ref, wh_ref, cw_ref, pre_ref, ya_ref, st_ref, s_ref, *, tm):
    i = pl.program_id(2)

    @pl.when(i == 0)
    def _():
        s_ref[0:SUBLANES, :] = pre_ref[0]

    x = x_ref[0]
    pb = _dot(x, wb_ref[...])
    u = _dot(x, wc_ref[...]) * _dot(x, wh_ref[...])
    s_ref[SUBLANES:SUBLANES + tm, :] = u
    u1 = s_ref[SUBLANES - 1:SUBLANES - 1 + tm, :]
    u2 = s_ref[SUBLANES - 2:SUBLANES - 2 + tm, :]
    cw = cw_ref[...]
    y = cw[0:1] * u2 + cw[1:2] * u1 + cw[2:3] * u
    ya_ref[0] = (pb * y).astype(ya_ref.dtype)
    last = s_ref[tm:tm + SUBLANES, :]
    s_ref[0:SUBLANES, :] = last
    st_ref[0] = last


def conv_proj(xb, wb, wc, wh, conv_w, prefix8, tm=512, tn=512):
    b, t, d = xb.shape
    dc = wb.shape[1]
    tm = min(tm, t)
    assert t % tm == 0 and dc % tn == 0 and tm >= SUBLANES
    return pl.pallas_call(
        functools.partial(_conv_proj_kernel, tm=tm),
        grid=(dc // tn, b, t // tm),
        in_specs=[pl.BlockSpec((1, tm, d), lambda j, bb, i: (bb, i, 0)),
                  pl.BlockSpec((d, tn), lambda j, bb, i: (0, j)),
                  pl.BlockSpec((d, tn), lambda j, bb, i: (0, j)),
                  pl.BlockSpec((d, tn), lambda j, bb, i: (0, j)),
                  pl.BlockSpec((CONV_WIDTH, tn), lambda j, bb, i: (0, j)),
                  pl.BlockSpec((1, SUBLANES, tn), lambda j, bb, i: (bb, 0, j))],
        out_specs=[pl.BlockSpec((1, tm, tn), lambda j, bb, i: (bb, i, j)),
                   pl.BlockSpec((1, SUBLANES, tn), lambda j, bb, i: (bb, 0, j))],
        out_shape=[jax.ShapeDtypeStruct((b, t, dc), BF16),
                   jax.ShapeDtypeStruct((b, SUBLANES, dc), F32)],
        scratch_shapes=[pltpu.VMEM((tm + SUBLANES, tn), F32)],
        compiler_params=_params(("parallel", "parallel", "arbitrary")),
        name="conv_proj",
    )(xb, wb, wc, wh, conv_w, prefix8)


def _step_conv_kernel(p_ref, s0_ref, s1_ref, cw_ref, ya_ref, u_ref, *, dc):
    pb = p_ref[:, 0:dc]
    u = p_ref[:, dc:2 * dc] * p_ref[:, 2 * dc:3 * dc]
    cw = cw_ref[...]
    y = cw[0:1] * s0_ref[...] + cw[1:2] * s1_ref[...] + cw[2:3] * u
    ya_ref[...] = (pb * y).astype(ya_ref.dtype)
    u_ref[...] = u


def step_conv(p, s0, s1, conv_w):
    n, dc3 = p.shape
    dc = dc3 // 3
    return pl.pallas_call(
        functools.partial(_step_conv_kernel, dc=dc),
        out_shape=[jax.ShapeDtypeStruct((n, dc), BF16), jax.ShapeDtypeStruct((n, dc), F32)],
        compiler_params=pltpu.CompilerParams(vmem_limit_bytes=VMEM_LIMIT),
        name="step_conv",
    )(p, s0, s1, conv_w)


def _cmp_kv_kernel(x_ref, wk_ref, wv_ref, ok_ref, ov_ref, *, nb):
    for kv, (w_ref, o_ref) in enumerate(((wk_ref, ok_ref), (wv_ref, ov_ref))):
        per_group = []
        for g in range(N_KV):
            cols = [x_ref[:, (l * 2 * N_KV + kv * N_KV + g) * HEAD_DIM:(l * 2 * N_KV + kv * N_KV + g + 1) * HEAD_DIM]
                    for l in range(CMP_BLOCK)]
            per_group.append(jnp.concatenate(cols, axis=1))
        lhs = jnp.concatenate(per_group, axis=0).astype(BF16)
        r = _dot(lhs, w_ref[...])
        for g in range(N_KV):
            o_ref[0, g] = r[g * nb:(g + 1) * nb]


def cmp_kv(blocks, wk, wv, nb=64):
    n_rows, width = blocks.shape
    assert n_rows % nb == 0
    nt = n_rows // nb
    return pl.pallas_call(
        functools.partial(_cmp_kv_kernel, nb=nb),
        grid=(nt,),
        in_specs=[pl.BlockSpec((nb, width), lambda i: (i, 0)),
                  pl.BlockSpec(wk.shape, lambda i: (0, 0)),
                  pl.BlockSpec(wv.shape, lambda i: (0, 0))],
        out_specs=[pl.BlockSpec((1, N_KV, nb, HEAD_DIM), lambda i: (i, 0, 0, 0))] * 2,
        out_shape=[jax.ShapeDtypeStruct((nt, N_KV, nb, HEAD_DIM), F32)] * 2,
        compiler_params=_params(("parallel",)),
        name="cmp_kv",
    )(blocks, wk, wv)


def _top_k_mask(val, iota, axis, k, size):
    sel = jnp.zeros(val.shape, F32)
    iota = iota.astype(F32)
    for _ in range(k):
        m = jnp.max(val, axis=axis, keepdims=True)
        first = jnp.min(jnp.where(val == m, iota, float(size)), axis=axis, keepdims=True)
        hit = iota == first
        sel = jnp.where(hit, 1.0, sel)
        val = jnp.where(hit, -jnp.inf, val)
    return sel


def _prompt_attn_kernel(q_ref, ng_ref, kc_ref, vc_ref, ks_ref, vs_ref, kw_ref, vw_ref, e_ref, o_ref,
                        m_ref, l_ref, acc_ref, oacc_ref, *, tq, tk, nc, ns):
    qt = pl.program_id(2)
    t0 = qt * tq
    qpos = t0 + lax.broadcasted_iota(I32, (tq, 1), 0)
    lane = lax.broadcasted_iota(I32, (tq, LANES), 1)
    gates = jax.nn.sigmoid(ng_ref[0])

    vis = ((lane + 1) * CMP_BLOCK - 1 <= qpos) & (lane < nc)
    visf = vis.astype(F32)
    kc = kc_ref[0, 0]
    vc = vc_ref[0, 0]
    imp = jnp.zeros((tq, LANES), F32)
    for h in range(HPG):
        qh = q_ref[0, :, h * HEAD_DIM:(h + 1) * HEAD_DIM]
        s = jnp.where(vis, _dot_nt(qh, kc) * ATTN_SCALE, NEG_INF)
        e = jnp.exp(s - jnp.max(s, axis=-1, keepdims=True))
        p = e / jnp.sum(e, axis=-1, keepdims=True) * visf
        imp = imp + p
        oacc_ref[h] = gates[:, h * 3:h * 3 + 1] * _dot(p.astype(BF16), vc)

    pair = imp + pltpu.roll(imp, LANES - 1, axis=1)
    blk = lane >> 1
    cand = ((lane & 1) == 0) & (blk < ns)
    cur = qpos >> (SEL_BLOCK.bit_length() - 1)
    forced = (blk == 0) | (blk == cur) | (blk == cur - 1)
    val = jnp.where(blk * SEL_BLOCK <= qpos, jnp.where(forced, FORCE_SCORE, pair), -1.0)
    val = jnp.where(cand, val, -jnp.inf)
    selb = _top_k_mask(val, lane, 1, SEL_TOPK, LANES).astype(BF16)

    def flash(k_ref, v_ref, kt_lo, kt_hi, mask_fn, branch):
        m_ref[...] = jnp.full(m_ref.shape, NEG_INF, F32)
        l_ref[...] = jnp.zeros(l_ref.shape, F32)
        acc_ref[...] = jnp.zeros(acc_ref.shape, F32)

        def body(kt, carry):
            k0 = pl.multiple_of(kt * tk, tk)
            k = k_ref[0, pl.ds(k0, tk), :]
            v = v_ref[0, pl.ds(k0, tk), :]
            mask = mask_fn(k0)
            for h in range(HPG):
                qh = q_ref[0, :, h * HEAD_DIM:(h + 1) * HEAD_DIM]
                s = jnp.where(mask, _dot_nt(qh, k) * ATTN_SCALE, NEG_INF)
                m_prev = m_ref[h][:, 0:1]
                m_next = jnp.maximum(m_prev, jnp.max(s, axis=-1, keepdims=True))
                alpha = jnp.exp(m_prev - m_next)
                p = jnp.where(mask, jnp.exp(s - m_next), 0.0)
                l_ref[h] = alpha * l_ref[h] + jnp.sum(p, axis=-1, keepdims=True)
                acc_ref[h] = alpha * acc_ref[h] + _dot(p.astype(BF16), v)
                m_ref[h] = jnp.broadcast_to(m_next, (tq, LANES))
            return carry

        lax.fori_loop(kt_lo, kt_hi, body, 0)
        for h in range(HPG):
            c = h * 3 + branch
            oacc_ref[h] = oacc_ref[h] + gates[:, c:c + 1] * (acc_ref[h] / l_ref[h][:, 0:1])

    kt_hi = (t0 + tq - 1) // tk + 1

    def sel_mask(k0):
        kpos = k0 + lax.broadcasted_iota(I32, (1, tk), 1)
        chosen = _dot(selb, e_ref[:, pl.ds(k0, tk)]) > 0.5
        return chosen & (kpos <= qpos)

    def win_mask(k0):
        kpos = k0 + lax.broadcasted_iota(I32, (1, tk), 1)
        return (kpos <= qpos) & (kpos >= qpos - WINDOW)

    flash(ks_ref, vs_ref, 0, kt_hi, sel_mask, 1)
    flash(kw_ref, vw_ref, jnp.maximum(t0 - WINDOW, 0) // tk, kt_hi, win_mask, 2)
    for h in range(HPG):
        o_ref[0, :, h * HEAD_DIM:(h + 1) * HEAD_DIM] = oacc_ref[h].astype(o_ref.dtype)


def prompt_attention(q, ng, kc, vc, sel_b, win_b, expand, tq=256, tk=256):
    b, t, _ = q.shape
    assert t % tq == 0 and t % tk == 0 and tq % tk == 0
    nc = t // CMP_BLOCK
    ns = max(-(-t // SEL_BLOCK), SEL_TOPK)
    assert nc <= LANES and 2 * ns <= LANES and ns * SEL_BLOCK == t
    gw = HPG * HEAD_DIM
    kern = functools.partial(_prompt_attn_kernel, tq=tq, tk=tk, nc=nc, ns=ns)
    kv_spec = lambda off: pl.BlockSpec((1, t, HEAD_DIM), lambda bb, g, i: (bb, 0, off + g))
    return pl.pallas_call(
        kern,
        grid=(b, N_KV, t // tq),
        in_specs=[pl.BlockSpec((1, tq, gw), lambda bb, g, i: (bb, i, g)),
                  pl.BlockSpec((1, tq, LANES), lambda bb, g, i: (bb, i, g)),
                  pl.BlockSpec((1, 1, LANES, HEAD_DIM), lambda bb, g, i: (bb, g, 0, 0)),
                  pl.BlockSpec((1, 1, LANES, HEAD_DIM), lambda bb, g, i: (bb, g, 0, 0)),
                  kv_spec(0), kv_spec(N_KV), kv_spec(0), kv_spec(N_KV),
                  pl.BlockSpec((LANES, t), lambda bb, g, i: (0, 0))],
        out_specs=pl.BlockSpec((1, tq, gw), lambda bb, g, i: (bb, i, g)),
        out_shape=jax.ShapeDtypeStruct(q.shape, BF16),
        scratch_shapes=[pltpu.VMEM((HPG, tq, LANES), F32), pltpu.VMEM((HPG, tq, LANES), F32),
                        pltpu.VMEM((HPG, tq, HEAD_DIM), F32), pltpu.VMEM((HPG, tq, HEAD_DIM), F32)],
        compiler_params=_params(("parallel", "parallel", "arbitrary")),
        name="prompt_attn",
    )(q, ng, kc, vc, sel_b, sel_b, win_b, win_b, expand)


def _step_attn_kernel(*refs, n_pages, past_len, nc, ns):
    (_, qb_ref, ng_ref, kc_ref, vc_ref) = refs[:5]
    pages = refs[5:5 + n_pages]
    (snew_ref, win_ref, wnew_ref, et_ref, o_ref, kbuf, vbuf, kwbuf, vwbuf) = refs[5 + n_pages:]
    gw = N_KV * HEAD_DIM
    qb = qb_ref[0]
    lane = lax.broadcasted_iota(I32, (LANES, LANES), 1)
    rowi = lax.broadcasted_iota(I32, (LANES, LANES), 0)

    def attend(k, v, mask):
        s = jnp.where(mask, _dot(k, qb) * ATTN_SCALE, NEG_INF)
        e = jnp.exp(s - jnp.max(s, axis=0, keepdims=True))
        p = e / jnp.sum(e, axis=0, keepdims=True) * mask.astype(F32)
        return p, _dot_tn(p.astype(BF16), v)

    vis = ((rowi + 1) * CMP_BLOCK - 1 <= past_len) & (rowi < nc)
    pc, oc = attend(kc_ref[0], vc_ref[0], vis)

    a = pc + pltpu.roll(pc, LANES - 1, axis=1)
    a = a + pltpu.roll(a, LANES - 2, axis=1)
    pair = a + pltpu.roll(a, LANES - 1, axis=0)
    blk = rowi >> 1
    cand = ((rowi & 1) == 0) & (blk < ns)
    cur = past_len // SEL_BLOCK
    forced = (blk == 0) | (blk == cur) | (blk == cur - 1)
    val = jnp.where(blk * SEL_BLOCK <= past_len, jnp.where(forced, FORCE_SCORE, pair), -1.0)
    val = jnp.where(cand, val, -jnp.inf)
    sel = _top_k_mask(val, rowi, 0, SEL_TOPK, LANES)
    sel = jnp.where(((lane & (HPG - 1)) == 0) & (lane < N_HEADS), sel, 0.0)
    selh = sel
    for r in range(1, HPG):
        selh = selh + pltpu.roll(sel, r, axis=1)

    nk = n_pages * PAGE_SIZE
    tail = lax.broadcasted_iota(I32, (2 * SUBLANES, gw), 0) == 0
    for p in range(n_pages):
        kbuf[p * PAGE_SIZE:(p + 1) * PAGE_SIZE, :] = pages[p][0, :, 0:gw].astype(BF16)
        vbuf[p * PAGE_SIZE:(p + 1) * PAGE_SIZE, :] = pages[p][0, :, gw:2 * gw].astype(BF16)
    kbuf[nk:nk + 2 * SUBLANES, :] = jnp.where(tail, snew_ref[0, :, 0:gw], 0.0).astype(BF16)
    vbuf[nk:nk + 2 * SUBLANES, :] = jnp.where(tail, snew_ref[0, :, gw:2 * gw], 0.0).astype(BF16)
    chosen = _dot(et_ref[...], selh.astype(BF16)) > 0.5
    _, osel = attend(kbuf[...], vbuf[...], chosen)

    nw = win_ref.shape[1]
    kwbuf[0:nw, :] = win_ref[0, :, 0:gw].astype(BF16)
    vwbuf[0:nw, :] = win_ref[0, :, gw:2 * gw].astype(BF16)
    kwbuf[nw:nw + 2 * SUBLANES, :] = jnp.where(tail, wnew_ref[0, :, 0:gw], 0.0).astype(BF16)
    vwbuf[nw:nw + 2 * SUBLANES, :] = jnp.where(tail, wnew_ref[0, :, gw:2 * gw], 0.0).astype(BF16)
    wrow = lax.broadcasted_iota(I32, (nw + 2 * SUBLANES, LANES), 0)
    _, owin = attend(kwbuf[...], vwbuf[...], wrow <= nw)

    gates = jax.nn.sigmoid(ng_ref[0])
    for h in range(N_HEADS):
        g, hh = divmod(h, HPG)
        c = g * LANES + hh * 3
        cols = slice(g * HEAD_DIM, (g + 1) * HEAD_DIM)
        o_h = (gates[:, c:c + 1] * oc[h:h + 1, cols] + gates[:, c + 1:c + 2] * osel[h:h + 1, cols]
               + gates[:, c + 2:c + 3] * owin[h:h + 1, cols])
        o_ref[0, :, h * HEAD_DIM:(h + 1) * HEAD_DIM] = o_h.astype(o_ref.dtype)


def step_attention(qblk, ng, kc, vc, sel_pool, page_table, sel_new, win_state, win_new, expand_t, past_len):
    n, n_pages = page_table.shape
    gw = N_KV * HEAD_DIM
    nk = n_pages * PAGE_SIZE
    nw = win_state.shape[1]
    nc = (past_len + 1) // CMP_BLOCK
    ns = max(-(-(past_len + 1) // SEL_BLOCK), SEL_TOPK)
    assert nc <= LANES and 2 * ns <= LANES and nw == WINDOW and past_len >= WINDOW
    kern = functools.partial(_step_attn_kernel, n_pages=n_pages, past_len=past_len, nc=nc, ns=ns)
    page_specs = [pl.BlockSpec((1, PAGE_SIZE, 2 * gw), functools.partial(lambda i, pt, p: (pt[i, p], 0, 0), p=p))
                  for p in range(n_pages)]
    grid_spec = pltpu.PrefetchScalarGridSpec(
        num_scalar_prefetch=1,
        grid=(n,),
        in_specs=[pl.BlockSpec((1, gw, LANES), lambda i, pt: (i, 0, 0)),
                  pl.BlockSpec((1, 1, N_KV * LANES), lambda i, pt: (i, 0, 0)),
                  pl.BlockSpec((1, LANES, gw), lambda i, pt: (i, 0, 0)),
                  pl.BlockSpec((1, LANES, gw), lambda i, pt: (i, 0, 0))]
        + page_specs
        + [pl.BlockSpec((1, 1, 2 * gw), lambda i, pt: (i, 0, 0)),
           pl.BlockSpec((1, nw, 2 * gw), lambda i, pt: (i, 0, 0)),
           pl.BlockSpec((1, 1, 2 * gw), lambda i, pt: (i, 0, 0)),
           pl.BlockSpec((nk + 2 * SUBLANES, LANES), lambda i, pt: (0, 0))],
        out_specs=pl.BlockSpec((1, 1, N_HEADS * HEAD_DIM), lambda i, pt: (i, 0, 0)),
        scratch_shapes=[pltpu.VMEM((nk + 2 * SUBLANES, gw), BF16), pltpu.VMEM((nk + 2 * SUBLANES, gw), BF16),
                        pltpu.VMEM((nw + 2 * SUBLANES, gw), BF16), pltpu.VMEM((nw + 2 * SUBLANES, gw), BF16)],
    )
    return pl.pallas_call(
        kern,
        grid_spec=grid_spec,
        out_shape=jax.ShapeDtypeStruct((n, 1, N_HEADS * HEAD_DIM), BF16),
        compiler_params=_params(("arbitrary",)),
        name="step_attn",
    )(page_table, qblk, ng, kc, vc, *([sel_pool] * n_pages), sel_new, win_state, win_new, expand_t)


def _merge_kernel(x_ref, ya_ref, ob_ref, wga_ref, wgb_ref, wco_ref, wao_ref, o_ref):
    x = x_ref[...]
    ga = jax.nn.sigmoid(_dot(x, wga_ref[...]))
    gb = jax.nn.sigmoid(_dot(x, wgb_ref[...]))
    y_a = _dot(ya_ref[...], wco_ref[...])
    y_b = _dot(ob_ref[...], wao_ref[...])
    o_ref[...] = (ga * y_a + gb * y_b).astype(o_ref.dtype)


def merge_mixers(xb, ya_in, ob, w_merge, w_conv_out, w_attn_out, tm=512, tn=512):
    m, d = xb.shape
    tm = min(tm, m)
    assert m % tm == 0 and d % tn == 0
    nj = d // tn
    row = lambda j, i: (i, 0)
    col = lambda j, i: (0, j)
    return pl.pallas_call(
        _merge_kernel,
        grid=(nj, m // tm),
        in_specs=[pl.BlockSpec((tm, d), row), pl.BlockSpec((tm, ya_in.shape[1]), row),
                  pl.BlockSpec((tm, ob.shape[1]), row),
                  pl.BlockSpec((d, tn), col), pl.BlockSpec((d, tn), lambda j, i: (0, j + nj)),
                  pl.BlockSpec((w_conv_out.shape[0], tn), col), pl.BlockSpec((w_attn_out.shape[0], tn), col)],
        out_specs=pl.BlockSpec((tm, tn), lambda j, i: (i, j)),
        out_shape=jax.ShapeDtypeStruct((m, d), BF16),
        compiler_params=_params(("parallel", "parallel")),
        name="merge_mixers",
    )(xb, ya_in, ob, w_merge, w_merge, w_conv_out, w_attn_out)


def _layer_norm(z, g, b):
    mu = jnp.mean(z, axis=-1, keepdims=True)
    zc = z - mu
    var = jnp.mean(zc * zc, axis=-1, keepdims=True)
    return zc * lax.rsqrt(var + LN_EPS) * g + b


def _out_proj_ln_kernel(m_ref, w_ref, x_ref, g_ref, b_ref, *rest, alpha):
    o_ref = rest[-1]
    z = alpha * x_ref[...] + _dot(m_ref[...], w_ref[...])
    o_ref[...] = _layer_norm(z, g_ref[...], b_ref[...])


def out_proj_ln(mixin, w_o, x, g, b, alpha, total_rows, row_offset, prev=None, tm=512):
    m, d = x.shape
    tm = min(tm, m)
    assert m % tm == 0 and row_offset % tm == 0
    off = row_offset // tm
    row = lambda i: (i, 0)
    fixed = lambda i: (0, 0)
    in_specs = [pl.BlockSpec((tm, d), row), pl.BlockSpec((d, d), fixed), pl.BlockSpec((tm, d), row),
                pl.BlockSpec((1, d), fixed), pl.BlockSpec((1, d), fixed)]
    args = [mixin, w_o, x, g, b]
    aliases = {}
    if prev is not None:
        in_specs.append(pl.BlockSpec(memory_space=pl.ANY))
        args.append(prev)
        aliases = {5: 0}
    return pl.pallas_call(
        functools.partial(_out_proj_ln_kernel, alpha=alpha),
        grid=(m // tm,),
        in_specs=in_specs,
        out_specs=pl.BlockSpec((tm, d), lambda i: (i + off, 0)),
        out_shape=jax.ShapeDtypeStruct((total_rows, d), F32),
        input_output_aliases=aliases,
        compiler_params=_params(("parallel",)),
        name="out_proj_ln",
    )(*args)


def _route_kernel(h_ref, wr_ref, rb_ref, u_ref, idx_ref, gate_ref, rank_ref, cnt_ref, run_ref, *, tm):
    i = pl.program_id(0)

    @pl.when(i == 0)
    def _():
        run_ref[...] = jnp.zeros(run_ref.shape, F32)

    logits = lax.dot_general(wr_ref[...], h_ref[...], (((1,), (1,)), ((), ())),
                             precision=lax.Precision.HIGHEST, preferred_element_type=F32)
    s = jax.nn.sigmoid(logits)
    bsc = s + rb_ref[...]
    gsz = N_EXPERTS // N_GROUPS
    x3 = bsc.reshape(N_GROUPS, gsz, tm)
    sub = lax.broadcasted_iota(I32, (N_GROUPS, gsz, tm), 1).astype(F32)
    m1 = jnp.max(x3, axis=1, keepdims=True)
    i1 = jnp.min(jnp.where(x3 == m1, sub, float(gsz)), axis=1, keepdims=True)
    m2 = jnp.max(jnp.where(sub == i1, -jnp.inf, x3), axis=1, keepdims=True)
    gsc = (m1 + m2).reshape(N_GROUPS, tm)
    giota = lax.broadcasted_iota(I32, (N_GROUPS, tm), 0)
    gsel = _top_k_mask(gsc, giota, 0, TOPK_GROUPS, N_GROUPS)
    emask = jnp.broadcast_to(gsel.reshape(N_GROUPS, 1, tm), (N_GROUPS, gsz, tm)).reshape(N_EXPERTS, tm) > 0.5
    val = jnp.where(emask, bsc, NEG_INF)
    eiota = lax.broadcasted_iota(I32, (N_EXPERTS, tm), 0).astype(F32)
    hits, idxs, sels = [], [], []
    for _ in range(MOE_TOPK):
        m = jnp.max(val, axis=0, keepdims=True)
        first = jnp.min(jnp.where(val == m, eiota, float(N_EXPERTS)), axis=0, keepdims=True)
        hit = eiota == first
        hits.append(hit)
        idxs.append(first.astype(I32))
        sels.append(jnp.sum(jnp.where(hit, s, 0.0), axis=0, keepdims=True))
        val = jnp.where(hit, -jnp.inf, val)
    den = sels[0]
    for k in range(1, MOE_TOPK):
        den = den + sels[k]
    onehot = hits[0].astype(F32)
    for k in range(1, MOE_TOPK):
        onehot = onehot + hits[k].astype(F32)
    before = _dot(onehot.astype(BF16), u_ref[...]) + run_ref[:, 0:1]
    pad = SUBLANES - MOE_TOPK
    zi = jnp.zeros((pad, tm), I32)
    zf = jnp.zeros((pad, tm), F32)
    ranks = [jnp.sum(jnp.where(hits[k], before, 0.0), axis=0, keepdims=True).astype(I32) for k in range(MOE_TOPK)]
    idx_ref[...] = jnp.concatenate(idxs + [zi], axis=0)
    gate_ref[...] = jnp.concatenate([sels[k] / den * ROUTED_SCALE for k in range(MOE_TOPK)] + [zf], axis=0)
    rank_ref[...] = jnp.concatenate(ranks + [zi], axis=0)
    run_ref[...] = run_ref[...] + jnp.sum(onehot, axis=1, keepdims=True)
    cnt_ref[...] = run_ref[...].astype(I32)


def route(h, w_router_t, router_bias, tm=ROUTE_TILE):
    t, d = h.shape
    assert t % tm == 0
    upper = (lax.broadcasted_iota(I32, (tm, tm), 0) < lax.broadcasted_iota(I32, (tm, tm), 1)).astype(BF16)
    tok = lambda i: (0, i)
    fixed = lambda i: (0, 0)
    return pl.pallas_call(
        functools.partial(_route_kernel, tm=tm),
        grid=(t // tm,),
        in_specs=[pl.BlockSpec((tm, d), lambda i: (i, 0)), pl.BlockSpec((N_EXPERTS, d), fixed),
                  pl.BlockSpec((N_EXPERTS, 1), fixed), pl.BlockSpec((tm, tm), fixed)],
        out_specs=[pl.BlockSpec((SUBLANES, tm), tok), pl.BlockSpec((SUBLANES, tm), tok),
                   pl.BlockSpec((SUBLANES, tm), tok), pl.BlockSpec((N_EXPERTS, LANES), fixed)],
        out_shape=[jax.ShapeDtypeStruct((SUBLANES, t), I32), jax.ShapeDtypeStruct((SUBLANES, t), F32),
                   jax.ShapeDtypeStruct((SUBLANES, t), I32), jax.ShapeDtypeStruct((N_EXPERTS, LANES), I32)],
        scratch_shapes=[pltpu.VMEM((N_EXPERTS, LANES), F32)],
        compiler_params=_params(("arbitrary",)),
        name="moe_route",
    )(h, w_router_t, router_bias.reshape(N_EXPERTS, 1), upper)


def _row_copy(src, src_row, dst, dst_row, sem):
    return pltpu.make_async_copy(src.at[pl.ds(src_row, 1)], dst.at[pl.ds(dst_row, 1)], sem)


def _dispatch_kernel(dest_ref, h_hbm, xs_in, xs_out, sem, *, tb):
    del xs_in
    base = pl.program_id(0) * tb

    def start(r, carry):
        for k in range(MOE_TOPK):
            _row_copy(h_hbm, base + r, xs_out, dest_ref[0, k, r], sem).start()
        return carry

    def wait(r, carry):
        for k in range(MOE_TOPK):
            _row_copy(h_hbm, base + r, xs_out, dest_ref[0, k, r], sem).wait()
        return carry

    lax.fori_loop(0, tb, start, 0)
    lax.fori_loop(0, tb, wait, 0)


def dispatch(h, dest_tiles, n_rows, tb=SCATTER_TILE):
    t, d = h.shape
    assert t % tb == 0
    zeros = jnp.zeros((n_rows, d), F32)
    return pl.pallas_call(
        functools.partial(_dispatch_kernel, tb=tb),
        grid=(t // tb,),
        in_specs=[pl.BlockSpec((1, SUBLANES, tb), lambda i: (i, 0, 0), memory_space=pltpu.SMEM),
                  pl.BlockSpec(memory_space=pl.ANY), pl.BlockSpec(memory_space=pl.ANY)],
        out_specs=pl.BlockSpec(memory_space=pl.ANY),
        out_shape=jax.ShapeDtypeStruct((n_rows, d), F32),
        scratch_shapes=[pltpu.SemaphoreType.DMA],
        input_output_aliases={2: 0},
        compiler_params=pltpu.CompilerParams(dimension_semantics=("arbitrary",), has_side_effects=True),
        name="moe_dispatch",
    )(dest_tiles, h, zeros)


def _expert_kernel(te_ref, nu_ref, x_ref, w1_ref, w3_ref, w2_ref, o_ref):
    i = pl.program_id(0)

    @pl.when(i < nu_ref[0])
    def _():
        x = x_ref[...].astype(BF16)
        a = _dot(x, w1_ref[0])
        h = (a * jax.nn.sigmoid(a)) * _dot(x, w3_ref[0])
        o_ref[...] = _dot(h.astype(BF16), w2_ref[0])

    @pl.when(i >= nu_ref[0])
    def _():
        o_ref[...] = jnp.zeros(o_ref.shape, o_ref.dtype)


def expert_mlp(xs, tile_expert, n_used, w1, w3, w2, tm=EXPERT_TILE):
    r, d = xs.shape
    f = w1.shape[2]
    assert r % tm == 0
    nt = r // tm
    rows = lambda i, te, nu: (jnp.minimum(i, nu[0] - 1), 0)
    wsel = lambda i, te, nu: (te[jnp.minimum(i, nu[0] - 1)], 0, 0)
    grid_spec = pltpu.PrefetchScalarGridSpec(
        num_scalar_prefetch=2,
        grid=(nt,),
        in_specs=[pl.BlockSpec((tm, d), rows), pl.BlockSpec((1, d, f), wsel),
                  pl.BlockSpec((1, d, f), wsel), pl.BlockSpec((1, f, d), wsel)],
        out_specs=pl.BlockSpec((tm, d), lambda i, te, nu: (i, 0)),
    )
    return pl.pallas_call(
        _expert_kernel,
        grid_spec=grid_spec,
        out_shape=jax.ShapeDtypeStruct((r, d), F32),
        compiler_params=_params(("arbitrary",)),
        name="expert_mlp",
    )(tile_expert, n_used, xs, w1, w3, w2)


def _combine_kernel(dest_ref, gate_ref, h_ref, sh_ref, g_ref, b_ref, o_hbm, y_ref, buf, sem, *, tb, alpha):

    def start(r, carry):
        for k in range(MOE_TOPK):
            _row_copy(o_hbm, dest_ref[0, k, r], buf.at[k], r, sem).start()
        return carry

    def wait(r, carry):
        for k in range(MOE_TOPK):
            _row_copy(o_hbm, dest_ref[0, k, r], buf.at[k], r, sem).wait()
        return carry

    lax.fori_loop(0, tb, start, 0)
    lax.fori_loop(0, tb, wait, 0)
    gate = gate_ref[...]
    routed = gate[:, 0:1] * buf[0]
    for k in range(1, MOE_TOPK):
        routed = routed + gate[:, k:k + 1] * buf[k]
    z = alpha * h_ref[...] + (routed + sh_ref[...])
    y_ref[...] = _layer_norm(z, g_ref[...], b_ref[...])


def combine_ln(dest_tiles, gate_t, h, shared, o_sorted, g, b, alpha, tb=COMBINE_TILE):
    t, d = h.shape
    assert t % tb == 0
    row = lambda i: (i, 0)
    fixed = lambda i: (0, 0)
    return pl.pallas_call(
        functools.partial(_combine_kernel, tb=tb, alpha=alpha),
        grid=(t // tb,),
        in_specs=[pl.BlockSpec((1, SUBLANES, tb), lambda i: (i, 0, 0), memory_space=pltpu.SMEM),
                  pl.BlockSpec((tb, SUBLANES), row), pl.BlockSpec((tb, d), row), pl.BlockSpec((tb, d), row),
                  pl.BlockSpec((1, d), fixed), pl.BlockSpec((1, d), fixed),
                  pl.BlockSpec(memory_space=pl.ANY)],
        out_specs=pl.BlockSpec((tb, d), row),
        out_shape=jax.ShapeDtypeStruct((t, d), F32),
        scratch_shapes=[pltpu.VMEM((MOE_TOPK, tb, d), F32), pltpu.SemaphoreType.DMA],
        compiler_params=_params(("arbitrary",)),
        name="moe_combine",
    )(dest_tiles, gate_t, h, shared, g, b, o_sorted)


def moe_ffn_ln(h1, w_router, router_bias, w1, w3, w2, ws1, ws3, ws2, g, b, alpha):
    t, d = h1.shape
    idx, gate, rank, cnt = route(h1, w_router.T, router_bias)
    counts = cnt[:, 0]
    padded = (counts + EXPERT_TILE - 1) // EXPERT_TILE * EXPERT_TILE
    pad_end = jnp.cumsum(padded)
    pad_start = pad_end - padded
    n_tiles = -(-(t * MOE_TOPK) // EXPERT_TILE) + N_EXPERTS
    tile_expert = jnp.minimum(
        jnp.searchsorted(pad_end, jnp.arange(n_tiles, dtype=I32) * EXPERT_TILE, side="right"),
        N_EXPERTS - 1).astype(I32)
    n_used = (pad_end[-1] // EXPERT_TILE).astype(I32).reshape(1)
    dest = pad_start[idx].astype(I32) + rank

    def tiles(a, tb):
        return a.reshape(SUBLANES, t // tb, tb).transpose(1, 0, 2)

    xs = dispatch(h1, tiles(dest, SCATTER_TILE), n_tiles * EXPERT_TILE)
    o_sorted = expert_mlp(xs, tile_expert, n_used, w1, w3, w2)
    n_dense = t // ROUTE_TILE
    shared = expert_mlp(h1, jnp.zeros((n_dense,), I32), jnp.full((1,), n_dense, I32),
                        ws1[None], ws3[None], ws2[None], tm=ROUTE_TILE)
    return combine_ln(tiles(dest, COMBINE_TILE), gate.T, h1, shared, o_sorted, g, b, alpha)


def _split_w_in(w_in, d_conv):
    d = w_in.shape[0]
    hd = N_HEADS * HEAD_DIM
    kvw = 2 * N_KV * HEAD_DIM
    o = 0
    parts = {}
    for name, width in (("b", d_conv), ("c", d_conv), ("h", d_conv), ("q", hd), ("cmp", kvw), ("sel", kvw),
                        ("win", kvw), ("nsa", 3 * N_HEADS), ("merge", 2 * d)):
        parts[name] = w_in[:, o:o + width]
        o += width
    nsa = parts["nsa"].reshape(d, N_KV, HPG * 3)
    parts["nsa"] = jnp.pad(nsa, ((0, 0), (0, 0), (0, LANES - HPG * 3))).reshape(d, N_KV * LANES)
    return {k: v.astype(BF16) for k, v in parts.items()}


def kernel(x_prompt, x_sample, cache_cmp_kv, cache_sel_kv, state_win_kv, state_conv, page_table,
           w_in, conv_w, w_phi_k, w_phi_v, w_conv_out, w_attn_out, w_o, ln1_g, ln1_b,
           w_router, router_bias, w_e_gate, w_e_up, w_e_down, w_s_gate, w_s_up, w_s_down,
           ln2_g, ln2_b):
    depth = w_in.shape[0]
    assert depth == 1
    alpha = (2.0 * depth) ** 0.25
    bp, tp, d = x_prompt.shape
    ns_, ts, _ = x_sample.shape
    assert ts == 1
    n_pages = page_table.shape[1]
    past_len = n_pages * PAGE_SIZE
    d_conv = conv_w.shape[2]
    kvw = 2 * N_KV * HEAD_DIM
    gw = N_KV * HEAD_DIM
    hd = N_HEADS * HEAD_DIM
    t_all = bp * tp + ns_

    w = _split_w_in(w_in[0], d_conv)
    cw = conv_w[0]
    wk_phi = w_phi_k[0].reshape(CMP_BLOCK * HEAD_DIM, HEAD_DIM).astype(BF16)
    wv_phi = w_phi_v[0].reshape(CMP_BLOCK * HEAD_DIM, HEAD_DIM).astype(BF16)
    wco = w_conv_out[0].astype(BF16)
    wao = w_attn_out[0].astype(BF16)
    wo = w_o[0].astype(BF16)
    g1, b1 = ln1_g[0].reshape(1, d), ln1_b[0].reshape(1, d)
    g2, b2 = ln2_g[0].reshape(1, d), ln2_b[0].reshape(1, d)

    xpb = x_prompt.astype(BF16)
    xp2 = xpb.reshape(bp * tp, d)
    ya_p, st_p = conv_proj(xpb, w["b"], w["c"], w["h"], cw, jnp.zeros((bp, SUBLANES, d_conv), F32))
    (q_p,) = matmul(xp2, w["q"], (BF16,))
    (cmp_p,) = matmul(xp2, w["cmp"], (F32,))
    sel_p, sel_pb = matmul(xp2, w["sel"], (F32, BF16))
    win_p, win_pb = matmul(xp2, w["win"], (F32, BF16))
    (ng_p,) = matmul(xp2, w["nsa"], (F32,))
    nc_p = tp // CMP_BLOCK
    kc_p, vc_p = cmp_kv(cmp_p.reshape(bp * nc_p, CMP_BLOCK * kvw), wk_phi, wv_phi, nb=nc_p)
    pad_c = lambda a: jnp.pad(a, ((0, 0), (0, 0), (0, LANES - a.shape[2]), (0, 0))).astype(BF16)
    blk_of_key = jnp.arange(tp, dtype=I32) // SEL_BLOCK
    expand = (jnp.arange(LANES, dtype=I32)[:, None] == 2 * blk_of_key[None, :]).astype(BF16)
    o_p = prompt_attention(q_p.reshape(bp, tp, hd), ng_p.reshape(bp, tp, N_KV * LANES), pad_c(kc_p), pad_c(vc_p),
                           sel_pb.reshape(bp, tp, kvw), win_pb.reshape(bp, tp, kvw), expand)
    mix_p = merge_mixers(xp2, ya_p.reshape(bp * tp, d_conv), o_p.reshape(bp * tp, hd), w["merge"], wco, wao)
    h1 = out_proj_ln(mix_p, wo, x_prompt.reshape(bp * tp, d), g1, b1, alpha, t_all, 0,
                     prev=jnp.zeros((t_all, d), F32))

    xs2 = x_sample.reshape(ns_, d)
    xsb = xs2.astype(BF16)
    w_conv3 = jnp.concatenate([w["b"], w["c"], w["h"]], axis=1)
    (pconv_s,) = matmul(xsb, w_conv3, (F32,))
    ya_s, u_s = step_conv(pconv_s, state_conv[0, :, 0], state_conv[0, :, 1], cw)
    (q_s,) = matmul(xsb, w["q"], (BF16,))
    (cmp_s,) = matmul(xsb, w["cmp"], (F32,))
    (sel_s,) = matmul(xsb, w["sel"], (F32,))
    (win_s,) = matmul(xsb, w["win"], (F32,))
    (ng_s,) = matmul(xsb, w["nsa"], (F32,))
    n_phys = cache_cmp_kv.shape[1]
    bpp = PAGE_SIZE // CMP_BLOCK
    pages_per_tile = 16
    assert n_phys % pages_per_tile == 0
    kc_all, vc_all = cmp_kv(cache_cmp_kv[0].reshape(n_phys * bpp, CMP_BLOCK * kvw), wk_phi, wv_phi,
                            nb=pages_per_tile * bpp)

    def per_sequence(a):
        a = a.reshape(n_phys // pages_per_tile, N_KV, pages_per_tile, bpp, HEAD_DIM)
        a = a.transpose(0, 2, 3, 1, 4).reshape(n_phys, bpp, gw)
        a = a[page_table].reshape(ns_, n_pages * bpp, gw)
        return jnp.pad(a, ((0, 0), (0, LANES - n_pages * bpp), (0, 0))).astype(BF16)

    head_group = jnp.arange(LANES, dtype=I32) // HPG
    qh = jnp.pad(q_s.reshape(ns_, N_HEADS, HEAD_DIM), ((0, 0), (0, LANES - N_HEADS), (0, 0)))
    qblk = jnp.where(jnp.arange(N_KV, dtype=I32)[None, :, None, None] == head_group[None, None, None, :],
                     qh.transpose(0, 2, 1)[:, None], 0).reshape(ns_, gw, LANES).astype(BF16)
    nk = n_pages * PAGE_SIZE
    key_row = jnp.arange(nk + 2 * SUBLANES, dtype=I32)
    expand_t = ((2 * (key_row // SEL_BLOCK))[:, None] == jnp.arange(LANES, dtype=I32)[None, :]) & (key_row <= nk)[:, None]
    o_s = step_attention(qblk, ng_s.reshape(ns_, 1, N_KV * LANES), per_sequence(kc_all), per_sequence(vc_all),
                         cache_sel_kv[0].reshape(n_phys, PAGE_SIZE, kvw), page_table,
                         sel_s.reshape(ns_, 1, kvw), state_win_kv[0].reshape(ns_, -1, kvw),
                         win_s.reshape(ns_, 1, kvw), expand_t.astype(BF16), past_len)
    mix_s = merge_mixers(xsb, ya_s, o_s.reshape(ns_, hd), w["merge"], wco, wao)
    h1 = out_proj_ln(mix_s, wo, xs2, g1, b1, alpha, t_all, bp * tp, prev=h1)

    y = moe_ffn_ln(h1, w_router[0], router_bias[0],
                   w_e_gate[0].astype(BF16), w_e_up[0].astype(BF16), w_e_down[0].astype(BF16),
                   w_s_gate[0].astype(BF16), w_s_up[0].astype(BF16), w_s_down[0].astype(BF16), g2, b2, alpha)

    kv6 = lambda a, n, t: a.reshape(1, n, t, 2, N_KV, HEAD_DIM)
    win_keep = min(WINDOW, tp)
    new_win_s = jnp.concatenate([state_win_kv[0][:, 1:], kv6(win_s, ns_, 1)[0]], axis=1)[None]
    new_conv_s = jnp.stack([state_conv[0, :, 1], u_s], axis=1)[None]
    return (y[:bp * tp].reshape(bp, tp, d), y[bp * tp:].reshape(ns_, ts, d),
            kv6(cmp_p, bp, tp), kv6(sel_p, bp, tp), kv6(win_p, bp, tp)[:, :, tp - win_keep:],
            st_p[None, :, SUBLANES - (CONV_WIDTH - 1):],
            kv6(cmp_s, ns_, 1), kv6(sel_s, ns_, 1), new_win_s, new_conv_s)
```

```python
import functools

import jax
import jax.numpy as jnp
from jax import lax
from jax.experimental import pallas as pl
from jax.experimental.pallas import tpu as pltpu

F32 = jnp.float32
BF16 = jnp.bfloat16
I32 = jnp.int32

N_HEADS = 16
HEAD_DIM = 128
N_KV = 4
HPG = N_HEADS // N_KV
CMP_BLOCK = 32
SEL_BLOCK = 64
SEL_TOPK = 8
WINDOW = 512
PAGE_SIZE = 128
CONV_WIDTH = 3
N_EXPERTS = 64
MOE_TOPK = 6
N_GROUPS = 8
TOPK_GROUPS = 4
ROUTED_SCALE = 2.5
LN_EPS = 1e-5
NEG_INF = -1e30
FORCE_SCORE = 1e4
ATTN_SCALE = HEAD_DIM ** -0.5

LANES = 128
SUBLANES = 8
VMEM_LIMIT = 56 * 1024 * 1024

EXPERT_TILE = 256
ROUTE_TILE = 384
COMBINE_TILE = 64
SCATTER_TILE = 128


def _dot(a, b):
    return jnp.dot(a, b, preferred_element_type=F32)


def _dot_nt(a, b):
    return lax.dot_general(a, b, (((1,), (1,)), ((), ())), preferred_element_type=F32)


def _dot_tn(a, b):
    return lax.dot_general(a, b, (((0,), (0,)), ((), ())), preferred_element_type=F32)


def _params(sem):
    return pltpu.CompilerParams(dimension_semantics=sem, vmem_limit_bytes=VMEM_LIMIT)


def _mm_kernel(x_ref, w_ref, *o_refs):
    r = _dot(x_ref[...], w_ref[...])
    for o_ref in o_refs:
        o_ref[...] = r.astype(o_ref.dtype)


def matmul(x, w, out_dtypes, tm=1024, tn=1024):
    m, k = x.shape
    n = w.shape[1]
    tm = min(tm, m)
    tn = min(tn, n)
    assert m % tm == 0 and n % tn == 0
    outs = pl.pallas_call(
        _mm_kernel,
        grid=(n // tn, m // tm),
        in_specs=[pl.BlockSpec((tm, k), lambda j, i: (i, 0)),
                  pl.BlockSpec((k, tn), lambda j, i: (0, j))],
        out_specs=[pl.BlockSpec((tm, tn), lambda j, i: (i, j)) for _ in out_dtypes],
        out_shape=[jax.ShapeDtypeStruct((m, n), dt) for dt in out_dtypes],
        compiler_params=_params(("parallel", "parallel")),
        name="proj_mm",
    )(x, w)
    return outs


def _conv_proj_kernel(x_ref, wb_ref, wc_ref, wh_ref, cw_ref, pre_ref, ya_ref, st_ref, s_ref, *, tm):
    i = pl.program_id(2)

    @pl.when(i == 0)
    def _():
        s_ref[0:SUBLANES, :] = pre_ref[0]

    x = x_ref[0]
    pb = _dot(x, wb_ref[...])
    u = _dot(x, wc_ref[...]) * _dot(x, wh_ref[...])
    s_ref[SUBLANES:SUBLANES + tm, :] = u
    u1 = s_ref[SUBLANES - 1:SUBLANES - 1 + tm, :]
    u2 = s_ref[SUBLANES - 2:SUBLANES - 2 + tm, :]
    cw = cw_ref[...]
    y = cw[0:1] * u2 + cw[1:2] * u1 + cw[2:3] * u
    ya_ref[0] = (pb * y).astype(ya_ref.dtype)
    last = s_ref[tm:tm + SUBLANES, :]
    s_ref[0:SUBLANES, :] = last
    st_ref[0] = last


def conv_proj(xb, wb, wc, wh, conv_w, prefix8, tm=512, tn=512):
    b, t, d = xb.shape
    dc = wb.shape[1]
    tm = min(tm, t)
    assert t % tm == 0 and dc % tn == 0 and tm >= SUBLANES
    return pl.pallas_call(
        functools.partial(_conv_proj_kernel, tm=tm),
        grid=(dc // tn, b, t // tm),
        in_specs=[pl.BlockSpec((1, tm, d), lambda j, bb, i: (bb, i, 0)),
                  pl.BlockSpec((d, tn), lambda j, bb, i: (0, j)),
                  pl.BlockSpec((d, tn), lambda j, bb, i: (0, j)),
                  pl.BlockSpec((d, tn), lambda j, bb, i: (0, j)),
                  pl.BlockSpec((CONV_WIDTH, tn), lambda j, bb, i: (0, j)),
                  pl.BlockSpec((1, SUBLANES, tn), lambda j, bb, i: (bb, 0, j))],
        out_specs=[pl.BlockSpec((1, tm, tn), lambda j, bb, i: (bb, i, j)),
                   pl.BlockSpec((1, SUBLANES, tn), lambda j, bb, i: (bb, 0, j))],
        out_shape=[jax.ShapeDtypeStruct((b, t, dc), BF16),
                   jax.ShapeDtypeStruct((b, SUBLANES, dc), F32)],
        scratch_shapes=[pltpu.VMEM((tm + SUBLANES, tn), F32)],
        compiler_params=_params(("parallel", "parallel", "arbitrary")),
        name="conv_proj",
    )(xb, wb, wc, wh, conv_w, prefix8)


def _step_conv_kernel(p_ref, s0_ref, s1_ref, cw_ref, ya_ref, u_ref, *, dc):
    pb = p_ref[:, 0:dc]
    u = p_ref[:, dc:2 * dc] * p_ref[:, 2 * dc:3 * dc]
    cw = cw_ref[...]
    y = cw[0:1] * s0_ref[...] + cw[1:2] * s1_ref[...] + cw[2:3] * u
    ya_ref[...] = (pb * y).astype(ya_ref.dtype)
    u_ref[...] = u


def step_conv(p, s0, s1, conv_w):
    n, dc3 = p.shape
    dc = dc3 // 3
    return pl.pallas_call(
        functools.partial(_step_conv_kernel, dc=dc),
        out_shape=[jax.ShapeDtypeStruct((n, dc), BF16), jax.ShapeDtypeStruct((n, dc), F32)],
        compiler_params=pltpu.CompilerParams(vmem_limit_bytes=VMEM_LIMIT),
        name="step_conv",
    )(p, s0, s1, conv_w)


def _cmp_kv_kernel(x_ref, wk_ref, wv_ref, ok_ref, ov_ref, *, nb):
    nslab = 2 * N_KV

    def rows(l, c):
        return x_ref[pl.ds(l * nslab + c, nb, stride=CMP_BLOCK * nslab), :]

    for kv, (w_ref, o_ref) in enumerate(((wk_ref, ok_ref), (wv_ref, ov_ref))):
        per_group = [jnp.concatenate([rows(l, kv * N_KV + g) for l in range(CMP_BLOCK)], axis=1)
                     for g in range(N_KV)]
        lhs = jnp.concatenate(per_group, axis=0).astype(BF16)
        r = _dot(lhs, w_ref[...])
        for g in range(N_KV):
            o_ref[0, g] = r[g * nb:(g + 1) * nb]


def cmp_kv(raw, wk, wv, nb):
    n_rows, width = raw.shape
    rows_per_step = nb * CMP_BLOCK * 2 * N_KV
    assert n_rows % rows_per_step == 0 and width == HEAD_DIM
    nt = n_rows // rows_per_step
    return pl.pallas_call(
        functools.partial(_cmp_kv_kernel, nb=nb),
        grid=(nt,),
        in_specs=[pl.BlockSpec((rows_per_step, width), lambda i: (i, 0)),
                  pl.BlockSpec(wk.shape, lambda i: (0, 0)),
                  pl.BlockSpec(wv.shape, lambda i: (0, 0))],
        out_specs=[pl.BlockSpec((1, N_KV, nb, HEAD_DIM), lambda i: (i, 0, 0, 0))] * 2,
        out_shape=[jax.ShapeDtypeStruct((nt, N_KV, nb, HEAD_DIM), F32)] * 2,
        compiler_params=_params(("parallel",)),
        name="cmp_kv",
    )(raw, wk, wv)


def _top_k_mask(val, iota, axis, k, size):
    sel = jnp.zeros(val.shape, F32)
    iota = iota.astype(F32)
    for _ in range(k):
        m = jnp.max(val, axis=axis, keepdims=True)
        first = jnp.min(jnp.where(val == m, iota, float(size)), axis=axis, keepdims=True)
        hit = iota == first
        sel = jnp.where(hit, 1.0, sel)
        val = jnp.where(hit, -jnp.inf, val)
    return sel


def _prompt_attn_kernel(q_ref, ng_ref, kc_ref, vc_ref, ks_ref, vs_ref, kw_ref, vw_ref, e_ref, o_ref,
                        m_ref, l_ref, acc_ref, oacc_ref, *, tq, tk, nc, ns):
    qt = pl.program_id(2)
    t0 = qt * tq
    qpos = t0 + lax.broadcasted_iota(I32, (tq, 1), 0)
    lane = lax.broadcasted_iota(I32, (tq, LANES), 1)
    gates = jax.nn.sigmoid(ng_ref[0])

    vis = ((lane + 1) * CMP_BLOCK - 1 <= qpos) & (lane < nc)
    visf = vis.astype(F32)
    kc = kc_ref[0, 0]
    vc = vc_ref[0, 0]
    imp = jnp.zeros((tq, LANES), F32)
    for h in range(HPG):
        qh = q_ref[0, :, h * HEAD_DIM:(h + 1) * HEAD_DIM]
        s = jnp.where(vis, _dot_nt(qh, kc) * ATTN_SCALE, NEG_INF)
        e = jnp.exp(s - jnp.max(s, axis=-1, keepdims=True))
        p = e / jnp.sum(e, axis=-1, keepdims=True) * visf
        imp = imp + p
        oacc_ref[h] = gates[:, h * 3:h * 3 + 1] * _dot(p.astype(BF16), vc)

    pair = imp + pltpu.roll(imp, LANES - 1, axis=1)
    blk = lane >> 1
    cand = ((lane & 1) == 0) & (blk < ns)
    cur = qpos >> (SEL_BLOCK.bit_length() - 1)
    forced = (blk == 0) | (blk == cur) | (blk == cur - 1)
    val = jnp.where(blk * SEL_BLOCK <= qpos, jnp.where(forced, FORCE_SCORE, pair), -1.0)
    val = jnp.where(cand, val, -jnp.inf)
    selb = _top_k_mask(val, lane, 1, SEL_TOPK, LANES).astype(BF16)

    def flash(k_ref, v_ref, kt_lo, kt_hi, mask_fn, branch):
        m_ref[...] = jnp.full(m_ref.shape, NEG_INF, F32)
        l_ref[...] = jnp.zeros(l_ref.shape, F32)
        acc_ref[...] = jnp.zeros(acc_ref.shape, F32)

        def body(kt, carry):
            k0 = pl.multiple_of(kt * tk, tk)
            k = k_ref[0, pl.ds(k0, tk), :]
            v = v_ref[0, pl.ds(k0, tk), :]
            mask = mask_fn(k0)
            for h in range(HPG):
                qh = q_ref[0, :, h * HEAD_DIM:(h + 1) * HEAD_DIM]
                s = jnp.where(mask, _dot_nt(qh, k) * ATTN_SCALE, NEG_INF)
                m_prev = m_ref[h]
                m_next = jnp.maximum(m_prev, jnp.max(s, axis=-1, keepdims=True))
                alpha = jnp.exp(m_prev - m_next)
                p = jnp.exp(s - jnp.tile(m_next, (1, tk // LANES)))
                psum = p[:, 0:LANES]
                for c in range(1, tk // LANES):
                    psum = psum + p[:, c * LANES:(c + 1) * LANES]
                l_ref[h] = alpha * l_ref[h] + psum
                acc_ref[h] = alpha * acc_ref[h] + _dot(p.astype(BF16), v)
                m_ref[h] = m_next
            return carry

        lax.fori_loop(kt_lo, kt_hi, body, 0)
        for h in range(HPG):
            c = h * 3 + branch
            l_tot = jnp.sum(l_ref[h], axis=-1, keepdims=True)
            oacc_ref[h] = oacc_ref[h] + gates[:, c:c + 1] * (acc_ref[h] / l_tot)

    kt_hi = (t0 + tq - 1) // tk + 1

    def sel_mask(k0):
        kpos = k0 + lax.broadcasted_iota(I32, (1, tk), 1)
        chosen = _dot(selb, e_ref[:, pl.ds(k0, tk)]) > 0.5
        return chosen & (kpos <= qpos)

    def win_mask(k0):
        kpos = k0 + lax.broadcasted_iota(I32, (1, tk), 1)
        return (kpos <= qpos) & (kpos >= qpos - WINDOW)

    flash(ks_ref, vs_ref, 0, kt_hi, sel_mask, 1)
    flash(kw_ref, vw_ref, jnp.maximum(t0 - WINDOW, 0) // tk, kt_hi, win_mask, 2)
    for h in range(HPG):
        o_ref[0, :, h * HEAD_DIM:(h + 1) * HEAD_DIM] = oacc_ref[h].astype(o_ref.dtype)


def prompt_attention(q, ng, kc, vc, sel_b, win_b, expand, tq=256, tk=256):
    b, t, _ = q.shape
    assert t % tq == 0 and t % tk == 0 and tq % tk == 0
    nc = t // CMP_BLOCK
    ns = max(-(-t // SEL_BLOCK), SEL_TOPK)
    assert nc <= LANES and 2 * ns <= LANES and ns * SEL_BLOCK == t
    gw = HPG * HEAD_DIM
    kern = functools.partial(_prompt_attn_kernel, tq=tq, tk=tk, nc=nc, ns=ns)
    kv_spec = lambda off: pl.BlockSpec((1, t, HEAD_DIM), lambda bb, g, i: (bb, 0, off + g))
    return pl.pallas_call(
        kern,
        grid=(b, N_KV, t // tq),
        in_specs=[pl.BlockSpec((1, tq, gw), lambda bb, g, i: (bb, i, g)),
                  pl.BlockSpec((1, tq, LANES), lambda bb, g, i: (bb, i, g)),
                  pl.BlockSpec((1, 1, LANES, HEAD_DIM), lambda bb, g, i: (bb, g, 0, 0)),
                  pl.BlockSpec((1, 1, LANES, HEAD_DIM), lambda bb, g, i: (bb, g, 0, 0)),
                  kv_spec(0), kv_spec(N_KV), kv_spec(0), kv_spec(N_KV),
                  pl.BlockSpec((LANES, t), lambda bb, g, i: (0, 0))],
        out_specs=pl.BlockSpec((1, tq, gw), lambda bb, g, i: (bb, i, g)),
        out_shape=jax.ShapeDtypeStruct(q.shape, BF16),
        scratch_shapes=[pltpu.VMEM((HPG, tq, LANES), F32), pltpu.VMEM((HPG, tq, LANES), F32),
                        pltpu.VMEM((HPG, tq, HEAD_DIM), F32), pltpu.VMEM((HPG, tq, HEAD_DIM), F32)],
        compiler_params=_params(("parallel", "parallel", "arbitrary")),
        name="prompt_attn",
    )(q, ng, kc, vc, sel_b, sel_b, win_b, win_b, expand)


def _step_attn_kernel(*refs, n_pages, past_len, nc, ns):
    (_, qb_ref, ng_ref, kc_ref, vc_ref) = refs[:5]
    pages = refs[5:5 + n_pages]
    (snew_ref, win_ref, wnew_ref, et_ref, o_ref, kbuf, vbuf, kwbuf, vwbuf) = refs[5 + n_pages:]
    gw = N_KV * HEAD_DIM
    qb = qb_ref[0]
    lane = lax.broadcasted_iota(I32, (LANES, LANES), 1)
    rowi = lax.broadcasted_iota(I32, (LANES, LANES), 0)

    def attend(k, v, mask):
        s = jnp.where(mask, _dot(k, qb) * ATTN_SCALE, NEG_INF)
        e = jnp.exp(s - jnp.max(s, axis=0, keepdims=True))
        p = e / jnp.sum(e, axis=0, keepdims=True) * mask.astype(F32)
        return p, _dot_tn(p.astype(BF16), v)

    vis = ((rowi + 1) * CMP_BLOCK - 1 <= past_len) & (rowi < nc)
    pc, oc = attend(kc_ref[0], vc_ref[0], vis)

    a = pc + pltpu.roll(pc, LANES - 1, axis=1)
    a = a + pltpu.roll(a, LANES - 2, axis=1)
    pair = a + pltpu.roll(a, LANES - 1, axis=0)
    blk = rowi >> 1
    cand = ((rowi & 1) == 0) & (blk < ns)
    cur = past_len // SEL_BLOCK
    forced = (blk == 0) | (blk == cur) | (blk == cur - 1)
    val = jnp.where(blk * SEL_BLOCK <= past_len, jnp.where(forced, FORCE_SCORE, pair), -1.0)
    val = jnp.where(cand, val, -jnp.inf)
    sel = _top_k_mask(val, rowi, 0, SEL_TOPK, LANES)
    sel = jnp.where(((lane & (HPG - 1)) == 0) & (lane < N_HEADS), sel, 0.0)
    selh = sel
    for r in range(1, HPG):
        selh = selh + pltpu.roll(sel, r, axis=1)

    nk = n_pages * PAGE_SIZE
    nslab = 2 * N_KV
    tail = lax.broadcasted_iota(I32, (2 * SUBLANES, gw), 0) == 0
    for p in range(n_pages):
        for g in range(N_KV):
            cols = slice(g * HEAD_DIM, (g + 1) * HEAD_DIM)
            kbuf[p * PAGE_SIZE:(p + 1) * PAGE_SIZE, cols] = pages[p][pl.ds(g, PAGE_SIZE, stride=nslab), :].astype(BF16)
            vbuf[p * PAGE_SIZE:(p + 1) * PAGE_SIZE, cols] = pages[p][pl.ds(N_KV + g, PAGE_SIZE, stride=nslab), :].astype(BF16)
    kbuf[nk:nk + 2 * SUBLANES, :] = jnp.where(tail, snew_ref[0, :, 0:gw], 0.0).astype(BF16)
    vbuf[nk:nk + 2 * SUBLANES, :] = jnp.where(tail, snew_ref[0, :, gw:2 * gw], 0.0).astype(BF16)
    chosen = _dot(et_ref[...], selh.astype(BF16)) > 0.5
    _, osel = attend(kbuf[...], vbuf[...], chosen)

    nw = win_ref.shape[0] // nslab
    for g in range(N_KV):
        cols = slice(g * HEAD_DIM, (g + 1) * HEAD_DIM)
        kwbuf[0:nw, cols] = win_ref[pl.ds(g, nw, stride=nslab), :].astype(BF16)
        vwbuf[0:nw, cols] = win_ref[pl.ds(N_KV + g, nw, stride=nslab), :].astype(BF16)
    kwbuf[nw:nw + 2 * SUBLANES, :] = jnp.where(tail, wnew_ref[0, :, 0:gw], 0.0).astype(BF16)
    vwbuf[nw:nw + 2 * SUBLANES, :] = jnp.where(tail, wnew_ref[0, :, gw:2 * gw], 0.0).astype(BF16)
    wrow = lax.broadcasted_iota(I32, (nw + 2 * SUBLANES, LANES), 0)
    _, owin = attend(kwbuf[...], vwbuf[...], wrow <= nw)

    gates = jax.nn.sigmoid(ng_ref[0])
    for h in range(N_HEADS):
        g, hh = divmod(h, HPG)
        c = g * LANES + hh * 3
        cols = slice(g * HEAD_DIM, (g + 1) * HEAD_DIM)
        o_h = (gates[:, c:c + 1] * oc[h:h + 1, cols] + gates[:, c + 1:c + 2] * osel[h:h + 1, cols]
               + gates[:, c + 2:c + 3] * owin[h:h + 1, cols])
        o_ref[0, :, h * HEAD_DIM:(h + 1) * HEAD_DIM] = o_h.astype(o_ref.dtype)


def step_attention(qblk, ng, kc, vc, sel_pool, page_table, sel_new, win_state, win_new, expand_t, past_len):
    n, n_pages = page_table.shape
    gw = N_KV * HEAD_DIM
    nslab = 2 * N_KV
    nk = n_pages * PAGE_SIZE
    nw = win_state.shape[0] // (n * nslab)
    nc = (past_len + 1) // CMP_BLOCK
    ns = max(-(-(past_len + 1) // SEL_BLOCK), SEL_TOPK)
    assert nc <= LANES and 2 * ns <= LANES and nw == WINDOW and past_len >= WINDOW
    kern = functools.partial(_step_attn_kernel, n_pages=n_pages, past_len=past_len, nc=nc, ns=ns)
    page_specs = [pl.BlockSpec((PAGE_SIZE * nslab, HEAD_DIM), functools.partial(lambda i, pt, p: (pt[i, p], 0), p=p))
                  for p in range(n_pages)]
    grid_spec = pltpu.PrefetchScalarGridSpec(
        num_scalar_prefetch=1,
        grid=(n,),
        in_specs=[pl.BlockSpec((1, gw, LANES), lambda i, pt: (i, 0, 0)),
                  pl.BlockSpec((1, 1, N_KV * LANES), lambda i, pt: (i, 0, 0)),
                  pl.BlockSpec((1, LANES, gw), lambda i, pt: (i, 0, 0)),
                  pl.BlockSpec((1, LANES, gw), lambda i, pt: (i, 0, 0))]
        + page_specs
        + [pl.BlockSpec((1, 1, 2 * gw), lambda i, pt: (i, 0, 0)),
           pl.BlockSpec((nw * nslab, HEAD_DIM), lambda i, pt: (i, 0)),
           pl.BlockSpec((1, 1, 2 * gw), lambda i, pt: (i, 0, 0)),
           pl.BlockSpec((nk + 2 * SUBLANES, LANES), lambda i, pt: (0, 0))],
        out_specs=pl.BlockSpec((1, 1, N_HEADS * HEAD_DIM), lambda i, pt: (i, 0, 0)),
        scratch_shapes=[pltpu.VMEM((nk + 2 * SUBLANES, gw), BF16), pltpu.VMEM((nk + 2 * SUBLANES, gw), BF16),
                        pltpu.VMEM((nw + 2 * SUBLANES, gw), BF16), pltpu.VMEM((nw + 2 * SUBLANES, gw), BF16)],
    )
    return pl.pallas_call(
        kern,
        grid_spec=grid_spec,
        out_shape=jax.ShapeDtypeStruct((n, 1, N_HEADS * HEAD_DIM), BF16),
        compiler_params=_params(("arbitrary",)),
        name="step_attn",
    )(page_table, qblk, ng, kc, vc, *([sel_pool] * n_pages), sel_new, win_state, win_new, expand_t)


def _merge_kernel(x_ref, ya_ref, ob_ref, wga_ref, wgb_ref, wco_ref, wao_ref, o_ref):
    x = x_ref[...]
    ga = jax.nn.sigmoid(_dot(x, wga_ref[...]))
    gb = jax.nn.sigmoid(_dot(x, wgb_ref[...]))
    y_a = _dot(ya_ref[...], wco_ref[...])
    y_b = _dot(ob_ref[...], wao_ref[...])
    o_ref[...] = (ga * y_a + gb * y_b).astype(o_ref.dtype)


def merge_mixers(xb, ya_in, ob, w_merge, w_conv_out, w_attn_out, tm=512, tn=512):
    m, d = xb.shape
    tm = min(tm, m)
    assert m % tm == 0 and d % tn == 0
    nj = d // tn
    row = lambda j, i: (i, 0)
    col = lambda j, i: (0, j)
    return pl.pallas_call(
        _merge_kernel,
        grid=(nj, m // tm),
        in_specs=[pl.BlockSpec((tm, d), row), pl.BlockSpec((tm, ya_in.shape[1]), row),
                  pl.BlockSpec((tm, ob.shape[1]), row),
                  pl.BlockSpec((d, tn), col), pl.BlockSpec((d, tn), lambda j, i: (0, j + nj)),
                  pl.BlockSpec((w_conv_out.shape[0], tn), col), pl.BlockSpec((w_attn_out.shape[0], tn), col)],
        out_specs=pl.BlockSpec((tm, tn), lambda j, i: (i, j)),
        out_shape=jax.ShapeDtypeStruct((m, d), BF16),
        compiler_params=_params(("parallel", "parallel")),
        name="merge_mixers",
    )(xb, ya_in, ob, w_merge, w_merge, w_conv_out, w_attn_out)


def _layer_norm(z, g, b):
    mu = jnp.mean(z, axis=-1, keepdims=True)
    zc = z - mu
    var = jnp.mean(zc * zc, axis=-1, keepdims=True)
    return zc * lax.rsqrt(var + LN_EPS) * g + b


def _out_proj_ln_kernel(m_ref, w_ref, x_ref, g_ref, b_ref, *rest, alpha):
    o_ref = rest[-1]
    z = alpha * x_ref[...] + _dot(m_ref[...], w_ref[...])
    o_ref[...] = _layer_norm(z, g_ref[...], b_ref[...])


def out_proj_ln(mixin, w_o, x, g, b, alpha, total_rows, row_offset, prev=None, tm=512):
    m, d = x.shape
    tm = min(tm, m)
    assert m % tm == 0 and row_offset % tm == 0
    off = row_offset // tm
    row = lambda i: (i, 0)
    fixed = lambda i: (0, 0)
    in_specs = [pl.BlockSpec((tm, d), row), pl.BlockSpec((d, d), fixed), pl.BlockSpec((tm, d), row),
                pl.BlockSpec((1, d), fixed), pl.BlockSpec((1, d), fixed)]
    args = [mixin, w_o, x, g, b]
    aliases = {}
    if prev is not None:
        in_specs.append(pl.BlockSpec(memory_space=pl.ANY))
        args.append(prev)
        aliases = {5: 0}
    return pl.pallas_call(
        functools.partial(_out_proj_ln_kernel, alpha=alpha),
        grid=(m // tm,),
        in_specs=in_specs,
        out_specs=pl.BlockSpec((tm, d), lambda i: (i + off, 0)),
        out_shape=jax.ShapeDtypeStruct((total_rows, d), F32),
        input_output_aliases=aliases,
        compiler_params=_params(("parallel",)),
        name="out_proj_ln",
    )(*args)


def _route_kernel(h_ref, wr_ref, rb_ref, u_ref, idx_ref, gate_ref, rank_ref, cnt_ref, run_ref, *, tm):
    i = pl.program_id(0)

    @pl.when(i == 0)
    def _():
        run_ref[...] = jnp.zeros(run_ref.shape, F32)

    logits = lax.dot_general(wr_ref[...], h_ref[...], (((1,), (1,)), ((), ())),
                             precision=lax.Precision.HIGHEST, preferred_element_type=F32)
    s = jax.nn.sigmoid(logits)
    bsc = s + rb_ref[...]
    gsz = N_EXPERTS // N_GROUPS
    x3 = bsc.reshape(N_GROUPS, gsz, tm)
    sub = lax.broadcasted_iota(I32, (N_GROUPS, gsz, tm), 1).astype(F32)
    m1 = jnp.max(x3, axis=1, keepdims=True)
    i1 = jnp.min(jnp.where(x3 == m1, sub, float(gsz)), axis=1, keepdims=True)
    m2 = jnp.max(jnp.where(sub == i1, -jnp.inf, x3), axis=1, keepdims=True)
    gsc = (m1 + m2).reshape(N_GROUPS, tm)
    giota = lax.broadcasted_iota(I32, (N_GROUPS, tm), 0)
    gsel = _top_k_mask(gsc, giota, 0, TOPK_GROUPS, N_GROUPS)
    emask = jnp.broadcast_to(gsel.reshape(N_GROUPS, 1, tm), (N_GROUPS, gsz, tm)).reshape(N_EXPERTS, tm) > 0.5
    val = jnp.where(emask, bsc, NEG_INF)
    eiota = lax.broadcasted_iota(I32, (N_EXPERTS, tm), 0).astype(F32)
    hits, idxs, sels = [], [], []
    for _ in range(MOE_TOPK):
        m = jnp.max(val, axis=0, keepdims=True)
        first = jnp.min(jnp.where(val == m, eiota, float(N_EXPERTS)), axis=0, keepdims=True)
        hit = eiota == first
        hits.append(hit)
        idxs.append(first.astype(I32))
        sels.append(jnp.sum(jnp.where(hit, s, 0.0), axis=0, keepdims=True))
        val = jnp.where(hit, -jnp.inf, val)
    den = sels[0]
    for k in range(1, MOE_TOPK):
        den = den + sels[k]
    onehot = hits[0].astype(F32)
    for k in range(1, MOE_TOPK):
        onehot = onehot + hits[k].astype(F32)
    before = _dot(onehot.astype(BF16), u_ref[...]) + run_ref[:, 0:1]
    pad = SUBLANES - MOE_TOPK
    zi = jnp.zeros((pad, tm), I32)
    zf = jnp.zeros((pad, tm), F32)
    ranks = [jnp.sum(jnp.where(hits[k], before, 0.0), axis=0, keepdims=True).astype(I32) for k in range(MOE_TOPK)]
    idx_ref[...] = jnp.concatenate(idxs + [zi], axis=0)
    gate_ref[...] = jnp.concatenate([sels[k] / den * ROUTED_SCALE for k in range(MOE_TOPK)] + [zf], axis=0)
    rank_ref[...] = jnp.concatenate(ranks + [zi], axis=0)
    run_ref[...] = run_ref[...] + jnp.sum(onehot, axis=1, keepdims=True)
    cnt_ref[...] = run_ref[...].astype(I32)


def route(h, w_router_t, router_bias, tm=ROUTE_TILE):
    t, d = h.shape
    assert t % tm == 0
    upper = (lax.broadcasted_iota(I32, (tm, tm), 0) < lax.broadcasted_iota(I32, (tm, tm), 1)).astype(BF16)
    tok = lambda i: (0, i)
    fixed = lambda i: (0, 0)
    return pl.pallas_call(
        functools.partial(_route_kernel, tm=tm),
        grid=(t // tm,),
        in_specs=[pl.BlockSpec((tm, d), lambda i: (i, 0)), pl.BlockSpec((N_EXPERTS, d), fixed),
                  pl.BlockSpec((N_EXPERTS, 1), fixed), pl.BlockSpec((tm, tm), fixed)],
        out_specs=[pl.BlockSpec((SUBLANES, tm), tok), pl.BlockSpec((SUBLANES, tm), tok),
                   pl.BlockSpec((SUBLANES, tm), tok), pl.BlockSpec((N_EXPERTS, LANES), fixed)],
        out_shape=[jax.ShapeDtypeStruct((SUBLANES, t), I32), jax.ShapeDtypeStruct((SUBLANES, t), F32),
                   jax.ShapeDtypeStruct((SUBLANES, t), I32), jax.ShapeDtypeStruct((N_EXPERTS, LANES), I32)],
        scratch_shapes=[pltpu.VMEM((N_EXPERTS, LANES), F32)],
        compiler_params=_params(("arbitrary",)),
        name="moe_route",
    )(h, w_router_t, router_bias.reshape(N_EXPERTS, 1), upper)


def _row_copy(src, src_row, dst, dst_row, sem):
    return pltpu.make_async_copy(src.at[pl.ds(src_row, 1)], dst.at[pl.ds(dst_row, 1)], sem)


def _dispatch_kernel(dest_ref, h_ref, xs_in, xs_out, sem, *, tb):
    del xs_in

    def start(r, carry):
        for k in range(MOE_TOPK):
            _row_copy(h_ref, r, xs_out, dest_ref[0, k, r], sem).start()
        return carry

    def wait(r, carry):
        for k in range(MOE_TOPK):
            _row_copy(h_ref, r, xs_out, dest_ref[0, k, r], sem).wait()
        return carry

    lax.fori_loop(0, tb, start, 0)
    lax.fori_loop(0, tb, wait, 0)


def dispatch(h, dest_tiles, n_rows, tb=SCATTER_TILE):
    t, d = h.shape
    assert t % tb == 0
    zeros = jnp.zeros((n_rows, d), F32)
    return pl.pallas_call(
        functools.partial(_dispatch_kernel, tb=tb),
        grid=(t // tb,),
        in_specs=[pl.BlockSpec((1, SUBLANES, tb), lambda i: (i, 0, 0), memory_space=pltpu.SMEM),
                  pl.BlockSpec((tb, d), lambda i: (i, 0)), pl.BlockSpec(memory_space=pl.ANY)],
        out_specs=pl.BlockSpec(memory_space=pl.ANY),
        out_shape=jax.ShapeDtypeStruct((n_rows, d), F32),
        scratch_shapes=[pltpu.SemaphoreType.DMA],
        input_output_aliases={2: 0},
        compiler_params=pltpu.CompilerParams(dimension_semantics=("arbitrary",), has_side_effects=True),
        name="moe_dispatch",
    )(dest_tiles, h, zeros)


def _expert_kernel(te_ref, nu_ref, x_ref, w1_ref, w3_ref, w2_ref, o_ref):
    i = pl.program_id(0)

    @pl.when(i < nu_ref[0])
    def _():
        x = x_ref[...].astype(BF16)
        a = _dot(x, w1_ref[0])
        h = (a * jax.nn.sigmoid(a)) * _dot(x, w3_ref[0])
        o_ref[...] = _dot(h.astype(BF16), w2_ref[0])

    @pl.when(i >= nu_ref[0])
    def _():
        o_ref[...] = jnp.zeros(o_ref.shape, o_ref.dtype)


def expert_mlp(xs, tile_expert, n_used, w1, w3, w2, tm=EXPERT_TILE):
    r, d = xs.shape
    f = w1.shape[2]
    assert r % tm == 0
    nt = r // tm
    rows = lambda i, te, nu: (jnp.minimum(i, nu[0] - 1), 0)
    wsel = lambda i, te, nu: (te[jnp.minimum(i, nu[0] - 1)], 0, 0)
    grid_spec = pltpu.PrefetchScalarGridSpec(
        num_scalar_prefetch=2,
        grid=(nt,),
        in_specs=[pl.BlockSpec((tm, d), rows), pl.BlockSpec((1, d, f), wsel),
                  pl.BlockSpec((1, d, f), wsel), pl.BlockSpec((1, f, d), wsel)],
        out_specs=pl.BlockSpec((tm, d), lambda i, te, nu: (i, 0)),
    )
    return pl.pallas_call(
        _expert_kernel,
        grid_spec=grid_spec,
        out_shape=jax.ShapeDtypeStruct((r, d), F32),
        compiler_params=_params(("arbitrary",)),
        name="expert_mlp",
    )(tile_expert, n_used, xs, w1, w3, w2)


def _combine_kernel(dest_ref, gate_ref, h_ref, sh_ref, g_ref, b_ref, o_hbm, y_ref, buf, sem, *, tb, alpha):

    def start(r, carry):
        for k in range(MOE_TOPK):
            _row_copy(o_hbm, dest_ref[0, k, r], buf.at[k], r, sem).start()
        return carry

    def wait(r, carry):
        for k in range(MOE_TOPK):
            _row_copy(o_hbm, dest_ref[0, k, r], buf.at[k], r, sem).wait()
        return carry

    lax.fori_loop(0, tb, start, 0)
    lax.fori_loop(0, tb, wait, 0)
    gate = gate_ref[...]
    routed = gate[:, 0:1] * buf[0]
    for k in range(1, MOE_TOPK):
        routed = routed + gate[:, k:k + 1] * buf[k]
    z = alpha * h_ref[...] + (routed + sh_ref[...])
    y_ref[...] = _layer_norm(z, g_ref[...], b_ref[...])


def combine_ln(dest_tiles, gate_t, h, shared, o_sorted, g, b, alpha, tb=COMBINE_TILE):
    t, d = h.shape
    assert t % tb == 0
    row = lambda i: (i, 0)
    fixed = lambda i: (0, 0)
    return pl.pallas_call(
        functools.partial(_combine_kernel, tb=tb, alpha=alpha),
        grid=(t // tb,),
        in_specs=[pl.BlockSpec((1, SUBLANES, tb), lambda i: (i, 0, 0), memory_space=pltpu.SMEM),
                  pl.BlockSpec((tb, SUBLANES), row), pl.BlockSpec((tb, d), row), pl.BlockSpec((tb, d), row),
                  pl.BlockSpec((1, d), fixed), pl.BlockSpec((1, d), fixed),
                  pl.BlockSpec(memory_space=pl.ANY)],
        out_specs=pl.BlockSpec((tb, d), row),
        out_shape=jax.ShapeDtypeStruct((t, d), F32),
        scratch_shapes=[pltpu.VMEM((MOE_TOPK, tb, d), F32), pltpu.SemaphoreType.DMA],
        compiler_params=_params(("arbitrary",)),
        name="moe_combine",
    )(dest_tiles, gate_t, h, shared, g, b, o_sorted)


def moe_ffn_ln(h1, w_router, router_bias, w1, w3, w2, ws1, ws3, ws2, g, b, alpha):
    t, d = h1.shape
    idx, gate, rank, cnt = route(h1, w_router.T, router_bias)
    counts = cnt[:, 0]
    padded = (counts + EXPERT_TILE - 1) // EXPERT_TILE * EXPERT_TILE
    e_ids = jnp.arange(N_EXPERTS, dtype=I32)
    pad_end = jnp.sum(jnp.where(e_ids[None, :] <= e_ids[:, None], padded[None, :], 0), axis=1)
    pad_start = pad_end - padded
    n_tiles = -(-(t * MOE_TOPK) // EXPERT_TILE) + N_EXPERTS
    tile_start = jnp.arange(n_tiles, dtype=I32) * EXPERT_TILE
    tile_expert = jnp.minimum(jnp.sum((pad_end[None, :] <= tile_start[:, None]).astype(I32), axis=1),
                              N_EXPERTS - 1)
    n_used = (pad_end[-1] // EXPERT_TILE).astype(I32).reshape(1)
    dest = jnp.sum(jnp.where(idx[:, :, None] == e_ids, pad_start, 0), axis=-1) + rank

    def tiles(a, tb):
        return a.reshape(SUBLANES, t // tb, tb).transpose(1, 0, 2)

    xs = dispatch(h1, tiles(dest, SCATTER_TILE), n_tiles * EXPERT_TILE)
    o_sorted = expert_mlp(xs, tile_expert, n_used, w1, w3, w2)
    n_dense = t // ROUTE_TILE
    shared = expert_mlp(h1, jnp.zeros((n_dense,), I32), jnp.full((1,), n_dense, I32),
                        ws1[None], ws3[None], ws2[None], tm=ROUTE_TILE)
    return combine_ln(tiles(dest, COMBINE_TILE), gate.T, h1, shared, o_sorted, g, b, alpha)


def _split_w_in(w_in, d_conv):
    d = w_in.shape[0]
    hd = N_HEADS * HEAD_DIM
    kvw = 2 * N_KV * HEAD_DIM
    o = 0
    parts = {}
    for name, width in (("b", d_conv), ("c", d_conv), ("h", d_conv), ("q", hd), ("cmp", kvw), ("sel", kvw),
                        ("win", kvw), ("nsa", 3 * N_HEADS), ("merge", 2 * d)):
        parts[name] = w_in[:, o:o + width]
        o += width
    nsa = parts["nsa"].reshape(d, N_KV, HPG * 3)
    parts["nsa"] = jnp.pad(nsa, ((0, 0), (0, 0), (0, LANES - HPG * 3))).reshape(d, N_KV * LANES)
    return {k: v.astype(BF16) for k, v in parts.items()}


def kernel(x_prompt, x_sample, cache_cmp_kv, cache_sel_kv, state_win_kv, state_conv, page_table,
           w_in, conv_w, w_phi_k, w_phi_v, w_conv_out, w_attn_out, w_o, ln1_g, ln1_b,
           w_router, router_bias, w_e_gate, w_e_up, w_e_down, w_s_gate, w_s_up, w_s_down,
           ln2_g, ln2_b):
    depth = w_in.shape[0]
    assert depth == 1
    alpha = (2.0 * depth) ** 0.25
    bp, tp, d = x_prompt.shape
    ns_, ts, _ = x_sample.shape
    assert ts == 1
    n_pages = page_table.shape[1]
    past_len = n_pages * PAGE_SIZE
    d_conv = conv_w.shape[2]
    kvw = 2 * N_KV * HEAD_DIM
    gw = N_KV * HEAD_DIM
    hd = N_HEADS * HEAD_DIM
    t_all = bp * tp + ns_

    w = _split_w_in(w_in[0], d_conv)
    cw = conv_w[0]
    wk_phi = w_phi_k[0].reshape(CMP_BLOCK * HEAD_DIM, HEAD_DIM).astype(BF16)
    wv_phi = w_phi_v[0].reshape(CMP_BLOCK * HEAD_DIM, HEAD_DIM).astype(BF16)
    wco = w_conv_out[0].astype(BF16)
    wao = w_attn_out[0].astype(BF16)
    wo = w_o[0].astype(BF16)
    g1, b1 = ln1_g[0].reshape(1, d), ln1_b[0].reshape(1, d)
    g2, b2 = ln2_g[0].reshape(1, d), ln2_b[0].reshape(1, d)

    xpb = x_prompt.astype(BF16)
    xp2 = xpb.reshape(bp * tp, d)
    ya_p, st_p = conv_proj(xpb, w["b"], w["c"], w["h"], cw, jnp.zeros((bp, SUBLANES, d_conv), F32))
    (q_p,) = matmul(xp2, w["q"], (BF16,))
    (cmp_p,) = matmul(xp2, w["cmp"], (F32,))
    sel_p, sel_pb = matmul(xp2, w["sel"], (F32, BF16))
    win_p, win_pb = matmul(xp2, w["win"], (F32, BF16))
    (ng_p,) = matmul(xp2, w["nsa"], (F32,))
    nc_p = tp // CMP_BLOCK
    slabs = lambda a: a.reshape(-1, HEAD_DIM)
    kc_p, vc_p = cmp_kv(slabs(cmp_p), wk_phi, wv_phi, nb=nc_p)
    pad_c = lambda a: jnp.pad(a, ((0, 0), (0, 0), (0, LANES - a.shape[2]), (0, 0))).astype(BF16)
    blk_of_key = jnp.arange(tp, dtype=I32) // SEL_BLOCK
    expand = (jnp.arange(LANES, dtype=I32)[:, None] == 2 * blk_of_key[None, :]).astype(BF16)
    o_p = prompt_attention(q_p.reshape(bp, tp, hd), ng_p.reshape(bp, tp, N_KV * LANES), pad_c(kc_p), pad_c(vc_p),
                           sel_pb.reshape(bp, tp, kvw), win_pb.reshape(bp, tp, kvw), expand)
    mix_p = merge_mixers(xp2, ya_p.reshape(bp * tp, d_conv), o_p.reshape(bp * tp, hd), w["merge"], wco, wao)
    h1 = out_proj_ln(mix_p, wo, x_prompt.reshape(bp * tp, d), g1, b1, alpha, t_all, 0,
                     prev=jnp.zeros((t_all, d), F32))

    xs2 = x_sample.reshape(ns_, d)
    xsb = xs2.astype(BF16)
    w_conv3 = jnp.concatenate([w["b"], w["c"], w["h"]], axis=1)
    (pconv_s,) = matmul(xsb, w_conv3, (F32,))
    ya_s, u_s = step_conv(pconv_s, state_conv[0, :, 0], state_conv[0, :, 1], cw)
    (q_s,) = matmul(xsb, w["q"], (BF16,))
    (cmp_s,) = matmul(xsb, w["cmp"], (F32,))
    (sel_s,) = matmul(xsb, w["sel"], (F32,))
    (win_s,) = matmul(xsb, w["win"], (F32,))
    (ng_s,) = matmul(xsb, w["nsa"], (F32,))
    n_phys = cache_cmp_kv.shape[1]
    bpp = PAGE_SIZE // CMP_BLOCK
    pages_per_tile = 16
    assert n_phys % pages_per_tile == 0
    kc_all, vc_all = cmp_kv(slabs(cache_cmp_kv), wk_phi, wv_phi, nb=pages_per_tile * bpp)

    def per_sequence(a):
        a = a.reshape(n_phys // pages_per_tile, N_KV, pages_per_tile, bpp, HEAD_DIM)
        a = a.transpose(0, 2, 3, 1, 4).reshape(n_phys, bpp, gw)
        a = a[page_table].reshape(ns_, n_pages * bpp, gw)
        return jnp.pad(a, ((0, 0), (0, LANES - n_pages * bpp), (0, 0))).astype(BF16)

    head_group = jnp.arange(LANES, dtype=I32) // HPG
    qh = jnp.pad(q_s.reshape(ns_, N_HEADS, HEAD_DIM), ((0, 0), (0, LANES - N_HEADS), (0, 0)))
    qblk = jnp.where(jnp.arange(N_KV, dtype=I32)[None, :, None, None] == head_group[None, None, None, :],
                     qh.transpose(0, 2, 1)[:, None], 0).reshape(ns_, gw, LANES).astype(BF16)
    nk = n_pages * PAGE_SIZE
    key_row = jnp.arange(nk + 2 * SUBLANES, dtype=I32)
    expand_t = ((2 * (key_row // SEL_BLOCK))[:, None] == jnp.arange(LANES, dtype=I32)[None, :]) & (key_row <= nk)[:, None]
    o_s = step_attention(qblk, ng_s.reshape(ns_, 1, N_KV * LANES), per_sequence(kc_all), per_sequence(vc_all),
                         slabs(cache_sel_kv), page_table, sel_s.reshape(ns_, 1, kvw), slabs(state_win_kv),
                         win_s.reshape(ns_, 1, kvw), expand_t.astype(BF16), past_len)
    mix_s = merge_mixers(xsb, ya_s, o_s.reshape(ns_, hd), w["merge"], wco, wao)
    h1 = out_proj_ln(mix_s, wo, xs2, g1, b1, alpha, t_all, bp * tp, prev=h1)

    y = moe_ffn_ln(h1, w_router[0], router_bias[0],
                   w_e_gate[0].astype(BF16), w_e_up[0].astype(BF16), w_e_down[0].astype(BF16),
                   w_s_gate[0].astype(BF16), w_s_up[0].astype(BF16), w_s_down[0].astype(BF16), g2, b2, alpha)

    kv6 = lambda a, n, t: a.reshape(1, n, t, 2, N_KV, HEAD_DIM)
    win_keep = min(WINDOW, tp)
    new_win_s = jnp.concatenate([state_win_kv[0][:, 1:], kv6(win_s, ns_, 1)[0]], axis=1)[None]
    new_conv_s = jnp.stack([state_conv[0, :, 1], u_s], axis=1)[None]
    return (y[:bp * tp].reshape(bp, tp, d), y[bp * tp:].reshape(ns_, ts, d),
            kv6(cmp_p, bp, tp), kv6(sel_p, bp, tp), kv6(win_p, bp, tp)[:, :, tp - win_keep:],
            st_p[None, :, SUBLANES - (CONV_WIDTH - 1):],
            kv6(cmp_s, ns_, 1), kv6(sel_s, ns_, 1), new_win_s, new_conv_s)
```

```python
import functools

import jax
import jax.numpy as jnp
from jax import lax
from jax.experimental import pallas as pl
from jax.experimental.pallas import tpu as pltpu

F32 = jnp.float32
BF16 = jnp.bfloat16
I32 = jnp.int32

N_HEADS = 16
HEAD_DIM = 128
N_KV = 4
HPG = N_HEADS // N_KV
CMP_BLOCK = 32
SEL_BLOCK = 64
SEL_TOPK = 8
WINDOW = 512
PAGE_SIZE = 128
CONV_WIDTH = 3
N_EXPERTS = 64
MOE_TOPK = 6
N_GROUPS = 8
TOPK_GROUPS = 4
ROUTED_SCALE = 2.5
LN_EPS = 1e-5
NEG_INF = -1e30
FORCE_SCORE = 1e4
ATTN_SCALE = HEAD_DIM ** -0.5
LOG2_E = 1.4426950408889634

LANES = 128
SUBLANES = 8
VMEM_LIMIT = 56 * 1024 * 1024

EXPERT_TILE = 256
ROUTE_TILE = 384
COMBINE_TILE = 64
SCATTER_TILE = 128
DMA_UNROLL = 8


def _dot(a, b):
    return jnp.dot(a, b, preferred_element_type=F32)


def _dot_nt(a, b):
    return lax.dot_general(a, b, (((1,), (1,)), ((), ())), preferred_element_type=F32)


def _dot_tn(a, b):
    return lax.dot_general(a, b, (((0,), (0,)), ((), ())), preferred_element_type=F32)


def _params(sem):
    return pltpu.CompilerParams(dimension_semantics=sem, vmem_limit_bytes=VMEM_LIMIT)


def _mm_kernel(x_ref, w_ref, *o_refs):
    r = _dot(x_ref[...], w_ref[...])
    for o_ref in o_refs:
        o_ref[...] = r.astype(o_ref.dtype)


def matmul(x, w, out_dtypes, tm=1024, tn=1024):
    m, k = x.shape
    n = w.shape[1]
    tm = min(tm, m)
    tn = min(tn, n)
    assert m % tm == 0 and n % tn == 0
    outs = pl.pallas_call(
        _mm_kernel,
        grid=(n // tn, m // tm),
        in_specs=[pl.BlockSpec((tm, k), lambda j, i: (i, 0)),
                  pl.BlockSpec((k, tn), lambda j, i: (0, j))],
        out_specs=[pl.BlockSpec((tm, tn), lambda j, i: (i, j)) for _ in out_dtypes],
        out_shape=[jax.ShapeDtypeStruct((m, n), dt) for dt in out_dtypes],
        compiler_params=_params(("parallel", "parallel")),
        name="proj_mm",
    )(x, w)
    return outs


def _conv_proj_kernel(x_ref, wb_ref, wc_ref, wh_ref, cw_ref, pre_ref, ya_ref, st_ref, s_ref, *, tm):
    i = pl.program_id(2)

    @pl.when(i == 0)
    def _():
        s_ref[0:SUBLANES, :] = pre_ref[0]

    x = x_ref[0]
    pb = _dot(x, wb_ref[...])
    u = _dot(x, wc_ref[...]) * _dot(x, wh_ref[...])
    s_ref[SUBLANES:SUBLANES + tm, :] = u
    u1 = s_ref[SUBLANES - 1:SUBLANES - 1 + tm, :]
    u2 = s_ref[SUBLANES - 2:SUBLANES - 2 + tm, :]
    cw = cw_ref[...]
    y = cw[0:1] * u2 + cw[1:2] * u1 + cw[2:3] * u
    ya_ref[0] = (pb * y).astype(ya_ref.dtype)
    last = s_ref[tm:tm + SUBLANES, :]
    s_ref[0:SUBLANES, :] = last
    st_ref[0] = last


def conv_proj(xb, wb, wc, wh, conv_w, prefix8, tm=512, tn=512):
    b, t, d = xb.shape
    dc = wb.shape[1]
    tm = min(tm, t)
    assert t % tm == 0 and dc % tn == 0 and tm >= SUBLANES
    return pl.pallas_call(
        functools.partial(_conv_proj_kernel, tm=tm),
        grid=(dc // tn, b, t // tm),
        in_specs=[pl.BlockSpec((1, tm, d), lambda j, bb, i: (bb, i, 0)),
                  pl.BlockSpec((d, tn), lambda j, bb, i: (0, j)),
                  pl.BlockSpec((d, tn), lambda j, bb, i: (0, j)),
                  pl.BlockSpec((d, tn), lambda j, bb, i: (0, j)),
                  pl.BlockSpec((CONV_WIDTH, tn), lambda j, bb, i: (0, j)),
                  pl.BlockSpec((1, SUBLANES, tn), lambda j, bb, i: (bb, 0, j))],
        out_specs=[pl.BlockSpec((1, tm, tn), lambda j, bb, i: (bb, i, j)),
                   pl.BlockSpec((1, SUBLANES, tn), lambda j, bb, i: (bb, 0, j))],
        out_shape=[jax.ShapeDtypeStruct((b, t, dc), BF16),
                   jax.ShapeDtypeStruct((b, SUBLANES, dc), F32)],
        scratch_shapes=[pltpu.VMEM((tm + SUBLANES, tn), F32)],
        compiler_params=_params(("parallel", "parallel", "arbitrary")),
        name="conv_proj",
    )(xb, wb, wc, wh, conv_w, prefix8)


def _step_conv_kernel(p_ref, s0_ref, s1_ref, cw_ref, ya_ref, u_ref, *, dc):
    pb = p_ref[:, 0:dc]
    u = p_ref[:, dc:2 * dc] * p_ref[:, 2 * dc:3 * dc]
    cw = cw_ref[...]
    y = cw[0:1] * s0_ref[...] + cw[1:2] * s1_ref[...] + cw[2:3] * u
    ya_ref[...] = (pb * y).astype(ya_ref.dtype)
    u_ref[...] = u


def step_conv(p, s0, s1, conv_w):
    n, dc3 = p.shape
    dc = dc3 // 3
    return pl.pallas_call(
        functools.partial(_step_conv_kernel, dc=dc),
        out_shape=[jax.ShapeDtypeStruct((n, dc), BF16), jax.ShapeDtypeStruct((n, dc), F32)],
        compiler_params=pltpu.CompilerParams(vmem_limit_bytes=VMEM_LIMIT),
        name="step_conv",
    )(p, s0, s1, conv_w)


def _cmp_kv_kernel(x_ref, wk_ref, wv_ref, ok_ref, ov_ref, *, nb):
    nslab = 2 * N_KV

    def rows(l, c):
        return x_ref[pl.ds(l * nslab + c, nb, stride=CMP_BLOCK * nslab), :].astype(BF16)

    for kv, (w_ref, o_ref) in enumerate(((wk_ref, ok_ref), (wv_ref, ov_ref))):
        per_group = [jnp.concatenate([rows(l, kv * N_KV + g) for l in range(CMP_BLOCK)], axis=1)
                     for g in range(N_KV)]
        lhs = jnp.concatenate(per_group, axis=0)
        r = _dot(lhs, w_ref[...])
        for g in range(N_KV):
            o_ref[0, g] = r[g * nb:(g + 1) * nb]


def cmp_kv(raw, wk, wv, nb):
    n_rows, width = raw.shape
    rows_per_step = nb * CMP_BLOCK * 2 * N_KV
    assert n_rows % rows_per_step == 0 and width == HEAD_DIM
    nt = n_rows // rows_per_step
    return pl.pallas_call(
        functools.partial(_cmp_kv_kernel, nb=nb),
        grid=(nt,),
        in_specs=[pl.BlockSpec((rows_per_step, width), lambda i: (i, 0)),
                  pl.BlockSpec(wk.shape, lambda i: (0, 0)),
                  pl.BlockSpec(wv.shape, lambda i: (0, 0))],
        out_specs=[pl.BlockSpec((1, N_KV, nb, HEAD_DIM), lambda i: (i, 0, 0, 0))] * 2,
        out_shape=[jax.ShapeDtypeStruct((nt, N_KV, nb, HEAD_DIM), F32)] * 2,
        compiler_params=_params(("parallel",)),
        name="cmp_kv",
    )(raw, wk, wv)


def _top_k_mask(val, iota, axis, k, size):
    sel = jnp.zeros(val.shape, F32)
    iota = iota.astype(F32)
    for _ in range(k):
        m = jnp.max(val, axis=axis, keepdims=True)
        first = jnp.min(jnp.where(val == m, iota, float(size)), axis=axis, keepdims=True)
        hit = iota == first
        sel = jnp.where(hit, 1.0, sel)
        val = jnp.where(hit, -jnp.inf, val)
    return sel


def _prompt_attn_kernel(q_ref, ng_ref, kc_ref, vc_ref, ks_ref, vs_ref, kw_ref, vw_ref, e_ref, o_ref,
                        m_ref, l_ref, acc_ref, oacc_ref, *, tq, tk, nc, ns):
    qt = pl.program_id(2)
    t0 = qt * tq
    qpos = t0 + lax.broadcasted_iota(I32, (tq, 1), 0)
    lane = lax.broadcasted_iota(I32, (tq, LANES), 1)
    gates = jax.nn.sigmoid(ng_ref[0])

    vis = ((lane + 1) * CMP_BLOCK - 1 <= qpos) & (lane < nc)
    visf = vis.astype(F32)
    kc = kc_ref[0, 0]
    vc = vc_ref[0, 0]
    imp = jnp.zeros((tq, LANES), F32)
    for h in range(HPG):
        qh = q_ref[0, :, h * HEAD_DIM:(h + 1) * HEAD_DIM]
        s = jnp.where(vis, _dot_nt(qh, kc) * ATTN_SCALE, NEG_INF)
        e = jnp.exp(s - jnp.max(s, axis=-1, keepdims=True))
        p = e / jnp.sum(e, axis=-1, keepdims=True) * visf
        imp = imp + p
        oacc_ref[h] = gates[:, h * 3:h * 3 + 1] * _dot(p.astype(BF16), vc)

    pair = imp + pltpu.roll(imp, LANES - 1, axis=1)
    blk = lane >> 1
    cand = ((lane & 1) == 0) & (blk < ns)
    cur = qpos >> (SEL_BLOCK.bit_length() - 1)
    forced = (blk == 0) | (blk == cur) | (blk == cur - 1)
    val = jnp.where(blk * SEL_BLOCK <= qpos, jnp.where(forced, FORCE_SCORE, pair), -1.0)
    val = jnp.where(cand, val, -jnp.inf)
    selb = _top_k_mask(val, lane, 1, SEL_TOPK, LANES).astype(BF16)

    def flash(k_ref, v_ref, kt_lo, kt_hi, mask_fn, branch):
        m_ref[...] = jnp.full(m_ref.shape, NEG_INF, F32)
        l_ref[...] = jnp.zeros(l_ref.shape, F32)
        acc_ref[...] = jnp.zeros(acc_ref.shape, F32)

        def body(kt, carry):
            k0 = pl.multiple_of(kt * tk, tk)
            k = k_ref[0, pl.ds(k0, tk), :]
            v = v_ref[0, pl.ds(k0, tk), :]
            mask = mask_fn(k0)
            for h in range(HPG):
                qh = q_ref[0, :, h * HEAD_DIM:(h + 1) * HEAD_DIM]
                s = jnp.where(mask, _dot_nt(qh, k) * (ATTN_SCALE * LOG2_E), NEG_INF)
                m_prev = m_ref[h]
                m_next = jnp.maximum(m_prev, jnp.max(s, axis=-1, keepdims=True))
                alpha = jnp.exp2(m_prev - m_next)
                p = jnp.exp2(s - jnp.tile(m_next, (1, tk // LANES)))
                psum = p[:, 0:LANES]
                for c in range(1, tk // LANES):
                    psum = psum + p[:, c * LANES:(c + 1) * LANES]
                l_ref[h] = alpha * l_ref[h] + psum
                acc_ref[h] = alpha * acc_ref[h] + _dot(p.astype(BF16), v)
                m_ref[h] = m_next
            return carry

        lax.fori_loop(kt_lo, kt_hi, body, 0)
        for h in range(HPG):
            c = h * 3 + branch
            l_tot = jnp.sum(l_ref[h], axis=-1, keepdims=True)
            oacc_ref[h] = oacc_ref[h] + gates[:, c:c + 1] * (acc_ref[h] / l_tot)

    kt_hi = (t0 + tq - 1) // tk + 1

    def sel_mask(k0):
        kpos = k0 + lax.broadcasted_iota(I32, (1, tk), 1)
        chosen = _dot(selb, e_ref[:, pl.ds(k0, tk)]) > 0.5
        return chosen & (kpos <= qpos)

    def win_mask(k0):
        kpos = k0 + lax.broadcasted_iota(I32, (1, tk), 1)
        return (kpos <= qpos) & (kpos >= qpos - WINDOW)

    flash(ks_ref, vs_ref, 0, kt_hi, sel_mask, 1)
    flash(kw_ref, vw_ref, jnp.maximum(t0 - WINDOW, 0) // tk, kt_hi, win_mask, 2)
    for h in range(HPG):
        o_ref[0, :, h * HEAD_DIM:(h + 1) * HEAD_DIM] = oacc_ref[h].astype(o_ref.dtype)


def prompt_attention(q, ng, kc, vc, sel_b, win_b, expand, tq=256, tk=256):
    b, t, _ = q.shape
    assert t % tq == 0 and t % tk == 0 and tq % tk == 0
    nc = t // CMP_BLOCK
    ns = max(-(-t // SEL_BLOCK), SEL_TOPK)
    assert nc <= LANES and 2 * ns <= LANES and ns * SEL_BLOCK == t
    gw = HPG * HEAD_DIM
    kern = functools.partial(_prompt_attn_kernel, tq=tq, tk=tk, nc=nc, ns=ns)
    kv_spec = lambda off: pl.BlockSpec((1, t, HEAD_DIM), lambda bb, g, i: (bb, 0, off + g))
    return pl.pallas_call(
        kern,
        grid=(b, N_KV, t // tq),
        in_specs=[pl.BlockSpec((1, tq, gw), lambda bb, g, i: (bb, i, g)),
                  pl.BlockSpec((1, tq, LANES), lambda bb, g, i: (bb, i, g)),
                  pl.BlockSpec((1, 1, LANES, HEAD_DIM), lambda bb, g, i: (bb, g, 0, 0)),
                  pl.BlockSpec((1, 1, LANES, HEAD_DIM), lambda bb, g, i: (bb, g, 0, 0)),
                  kv_spec(0), kv_spec(N_KV), kv_spec(0), kv_spec(N_KV),
                  pl.BlockSpec((LANES, t), lambda bb, g, i: (0, 0))],
        out_specs=pl.BlockSpec((1, tq, gw), lambda bb, g, i: (bb, i, g)),
        out_shape=jax.ShapeDtypeStruct(q.shape, BF16),
        scratch_shapes=[pltpu.VMEM((HPG, tq, LANES), F32), pltpu.VMEM((HPG, tq, LANES), F32),
                        pltpu.VMEM((HPG, tq, HEAD_DIM), F32), pltpu.VMEM((HPG, tq, HEAD_DIM), F32)],
        compiler_params=_params(("parallel", "parallel", "arbitrary")),
        name="prompt_attn",
    )(q, ng, kc, vc, sel_b, sel_b, win_b, win_b, expand)


def _step_attn_kernel(*refs, n_pages, past_len, nc, ns):
    (_, qb_ref, ng_ref, kc_ref, vc_ref) = refs[:5]
    pages = refs[5:5 + n_pages]
    (snew_ref, win_ref, wnew_ref, et_ref, o_ref, kbuf, vbuf, kwbuf, vwbuf) = refs[5 + n_pages:]
    gw = N_KV * HEAD_DIM
    qb = qb_ref[0]
    lane = lax.broadcasted_iota(I32, (LANES, LANES), 1)
    rowi = lax.broadcasted_iota(I32, (LANES, LANES), 0)

    def attend(k, v, mask):
        s = jnp.where(mask, _dot(k, qb) * ATTN_SCALE, NEG_INF)
        e = jnp.exp(s - jnp.max(s, axis=0, keepdims=True))
        p = e / jnp.sum(e, axis=0, keepdims=True) * mask.astype(F32)
        return p, _dot_tn(p.astype(BF16), v)

    vis = ((rowi + 1) * CMP_BLOCK - 1 <= past_len) & (rowi < nc)
    pc, oc = attend(kc_ref[0], vc_ref[0], vis)

    a = pc + pltpu.roll(pc, LANES - 1, axis=1)
    a = a + pltpu.roll(a, LANES - 2, axis=1)
    pair = a + pltpu.roll(a, LANES - 1, axis=0)
    blk = rowi >> 1
    cand = ((rowi & 1) == 0) & (blk < ns)
    cur = past_len // SEL_BLOCK
    forced = (blk == 0) | (blk == cur) | (blk == cur - 1)
    val = jnp.where(blk * SEL_BLOCK <= past_len, jnp.where(forced, FORCE_SCORE, pair), -1.0)
    val = jnp.where(cand, val, -jnp.inf)
    sel = _top_k_mask(val, rowi, 0, SEL_TOPK, LANES)
    sel = jnp.where(((lane & (HPG - 1)) == 0) & (lane < N_HEADS), sel, 0.0)
    selh = sel
    for r in range(1, HPG):
        selh = selh + pltpu.roll(sel, r, axis=1)

    nk = n_pages * PAGE_SIZE
    nslab = 2 * N_KV
    tail = lax.broadcasted_iota(I32, (2 * SUBLANES, gw), 0) == 0
    for p in range(n_pages):
        for g in range(N_KV):
            cols = slice(g * HEAD_DIM, (g + 1) * HEAD_DIM)
            kbuf[p * PAGE_SIZE:(p + 1) * PAGE_SIZE, cols] = pages[p][pl.ds(g, PAGE_SIZE, stride=nslab), :].astype(BF16)
            vbuf[p * PAGE_SIZE:(p + 1) * PAGE_SIZE, cols] = pages[p][pl.ds(N_KV + g, PAGE_SIZE, stride=nslab), :].astype(BF16)
    kbuf[nk:nk + 2 * SUBLANES, :] = jnp.where(tail, snew_ref[0, :, 0:gw], 0.0).astype(BF16)
    vbuf[nk:nk + 2 * SUBLANES, :] = jnp.where(tail, snew_ref[0, :, gw:2 * gw], 0.0).astype(BF16)
    chosen = _dot(et_ref[...], selh.astype(BF16)) > 0.5
    _, osel = attend(kbuf[...], vbuf[...], chosen)

    nw = win_ref.shape[0] // nslab
    for g in range(N_KV):
        cols = slice(g * HEAD_DIM, (g + 1) * HEAD_DIM)
        kwbuf[0:nw, cols] = win_ref[pl.ds(g, nw, stride=nslab), :].astype(BF16)
        vwbuf[0:nw, cols] = win_ref[pl.ds(N_KV + g, nw, stride=nslab), :].astype(BF16)
    kwbuf[nw:nw + 2 * SUBLANES, :] = jnp.where(tail, wnew_ref[0, :, 0:gw], 0.0).astype(BF16)
    vwbuf[nw:nw + 2 * SUBLANES, :] = jnp.where(tail, wnew_ref[0, :, gw:2 * gw], 0.0).astype(BF16)
    wrow = lax.broadcasted_iota(I32, (nw + 2 * SUBLANES, LANES), 0)
    _, owin = attend(kwbuf[...], vwbuf[...], wrow <= nw)

    gates = jax.nn.sigmoid(ng_ref[0])
    for h in range(N_HEADS):
        g, hh = divmod(h, HPG)
        c = g * LANES + hh * 3
        cols = slice(g * HEAD_DIM, (g + 1) * HEAD_DIM)
        o_h = (gates[:, c:c + 1] * oc[h:h + 1, cols] + gates[:, c + 1:c + 2] * osel[h:h + 1, cols]
               + gates[:, c + 2:c + 3] * owin[h:h + 1, cols])
        o_ref[0, :, h * HEAD_DIM:(h + 1) * HEAD_DIM] = o_h.astype(o_ref.dtype)


def step_attention(qblk, ng, kc, vc, sel_pool, page_table, sel_new, win_state, win_new, expand_t, past_len):
    n, n_pages = page_table.shape
    gw = N_KV * HEAD_DIM
    nslab = 2 * N_KV
    nk = n_pages * PAGE_SIZE
    nw = win_state.shape[0] // (n * nslab)
    nc = (past_len + 1) // CMP_BLOCK
    ns = max(-(-(past_len + 1) // SEL_BLOCK), SEL_TOPK)
    assert nc <= LANES and 2 * ns <= LANES and nw == WINDOW and past_len >= WINDOW
    kern = functools.partial(_step_attn_kernel, n_pages=n_pages, past_len=past_len, nc=nc, ns=ns)
    page_specs = [pl.BlockSpec((PAGE_SIZE * nslab, HEAD_DIM), functools.partial(lambda i, pt, p: (pt[i, p], 0), p=p))
                  for p in range(n_pages)]
    grid_spec = pltpu.PrefetchScalarGridSpec(
        num_scalar_prefetch=1,
        grid=(n,),
        in_specs=[pl.BlockSpec((1, gw, LANES), lambda i, pt: (i, 0, 0)),
                  pl.BlockSpec((1, 1, N_KV * LANES), lambda i, pt: (i, 0, 0)),
                  pl.BlockSpec((1, LANES, gw), lambda i, pt: (i, 0, 0)),
                  pl.BlockSpec((1, LANES, gw), lambda i, pt: (i, 0, 0))]
        + page_specs
        + [pl.BlockSpec((1, 1, 2 * gw), lambda i, pt: (i, 0, 0)),
           pl.BlockSpec((nw * nslab, HEAD_DIM), lambda i, pt: (i, 0)),
           pl.BlockSpec((1, 1, 2 * gw), lambda i, pt: (i, 0, 0)),
           pl.BlockSpec((nk + 2 * SUBLANES, LANES), lambda i, pt: (0, 0))],
        out_specs=pl.BlockSpec((1, 1, N_HEADS * HEAD_DIM), lambda i, pt: (i, 0, 0)),
        scratch_shapes=[pltpu.VMEM((nk + 2 * SUBLANES, gw), BF16), pltpu.VMEM((nk + 2 * SUBLANES, gw), BF16),
                        pltpu.VMEM((nw + 2 * SUBLANES, gw), BF16), pltpu.VMEM((nw + 2 * SUBLANES, gw), BF16)],
    )
    return pl.pallas_call(
        kern,
        grid_spec=grid_spec,
        out_shape=jax.ShapeDtypeStruct((n, 1, N_HEADS * HEAD_DIM), BF16),
        compiler_params=_params(("arbitrary",)),
        name="step_attn",
    )(page_table, qblk, ng, kc, vc, *([sel_pool] * n_pages), sel_new, win_state, win_new, expand_t)


def _merge_kernel(x_ref, ya_ref, ob_ref, wga_ref, wgb_ref, wco_ref, wao_ref, o_ref):
    x = x_ref[...]
    ga = jax.nn.sigmoid(_dot(x, wga_ref[...]))
    gb = jax.nn.sigmoid(_dot(x, wgb_ref[...]))
    y_a = _dot(ya_ref[...], wco_ref[...])
    y_b = _dot(ob_ref[...], wao_ref[...])
    o_ref[...] = (ga * y_a + gb * y_b).astype(o_ref.dtype)


def merge_mixers(xb, ya_in, ob, w_merge, w_conv_out, w_attn_out, tm=512, tn=512):
    m, d = xb.shape
    tm = min(tm, m)
    assert m % tm == 0 and d % tn == 0
    nj = d // tn
    row = lambda j, i: (i, 0)
    col = lambda j, i: (0, j)
    return pl.pallas_call(
        _merge_kernel,
        grid=(nj, m // tm),
        in_specs=[pl.BlockSpec((tm, d), row), pl.BlockSpec((tm, ya_in.shape[1]), row),
                  pl.BlockSpec((tm, ob.shape[1]), row),
                  pl.BlockSpec((d, tn), col), pl.BlockSpec((d, tn), lambda j, i: (0, j + nj)),
                  pl.BlockSpec((w_conv_out.shape[0], tn), col), pl.BlockSpec((w_attn_out.shape[0], tn), col)],
        out_specs=pl.BlockSpec((tm, tn), lambda j, i: (i, j)),
        out_shape=jax.ShapeDtypeStruct((m, d), BF16),
        compiler_params=_params(("parallel", "parallel")),
        name="merge_mixers",
    )(xb, ya_in, ob, w_merge, w_merge, w_conv_out, w_attn_out)


def _layer_norm(z, g, b):
    mu = jnp.mean(z, axis=-1, keepdims=True)
    zc = z - mu
    var = jnp.mean(zc * zc, axis=-1, keepdims=True)
    return zc * lax.rsqrt(var + LN_EPS) * g + b


def _out_proj_ln_kernel(m_ref, w_ref, x_ref, g_ref, b_ref, *rest, alpha):
    o_ref = rest[-1]
    z = alpha * x_ref[...] + _dot(m_ref[...], w_ref[...])
    o_ref[...] = _layer_norm(z, g_ref[...], b_ref[...])


def out_proj_ln(mixin, w_o, x, g, b, alpha, total_rows, row_offset, prev=None, tm=512):
    m, d = x.shape
    tm = min(tm, m)
    assert m % tm == 0 and row_offset % tm == 0
    off = row_offset // tm
    row = lambda i: (i, 0)
    fixed = lambda i: (0, 0)
    in_specs = [pl.BlockSpec((tm, d), row), pl.BlockSpec((d, d), fixed), pl.BlockSpec((tm, d), row),
                pl.BlockSpec((1, d), fixed), pl.BlockSpec((1, d), fixed)]
    args = [mixin, w_o, x, g, b]
    aliases = {}
    if prev is not None:
        in_specs.append(pl.BlockSpec(memory_space=pl.ANY))
        args.append(prev)
        aliases = {5: 0}
    return pl.pallas_call(
        functools.partial(_out_proj_ln_kernel, alpha=alpha),
        grid=(m // tm,),
        in_specs=in_specs,
        out_specs=pl.BlockSpec((tm, d), lambda i: (i + off, 0)),
        out_shape=jax.ShapeDtypeStruct((total_rows, d), F32),
        input_output_aliases=aliases,
        compiler_params=_params(("parallel",)),
        name="out_proj_ln",
    )(*args)


def _route_kernel(h_ref, wr_ref, rb_ref, u_ref, idx_ref, gate_ref, rank_ref, cnt_ref, run_ref, *, tm):
    i = pl.program_id(0)

    @pl.when(i == 0)
    def _():
        run_ref[...] = jnp.zeros(run_ref.shape, F32)

    logits = lax.dot_general(wr_ref[...], h_ref[...], (((1,), (1,)), ((), ())),
                             precision=lax.Precision.HIGHEST, preferred_element_type=F32)
    s = jax.nn.sigmoid(logits)
    bsc = s + rb_ref[...]
    gsz = N_EXPERTS // N_GROUPS
    x3 = bsc.reshape(N_GROUPS, gsz, tm)
    sub = lax.broadcasted_iota(I32, (N_GROUPS, gsz, tm), 1).astype(F32)
    m1 = jnp.max(x3, axis=1, keepdims=True)
    i1 = jnp.min(jnp.where(x3 == m1, sub, float(gsz)), axis=1, keepdims=True)
    m2 = jnp.max(jnp.where(sub == i1, -jnp.inf, x3), axis=1, keepdims=True)
    gsc = (m1 + m2).reshape(N_GROUPS, tm)
    giota = lax.broadcasted_iota(I32, (N_GROUPS, tm), 0)
    gsel = _top_k_mask(gsc, giota, 0, TOPK_GROUPS, N_GROUPS)
    emask = jnp.broadcast_to(gsel.reshape(N_GROUPS, 1, tm), (N_GROUPS, gsz, tm)).reshape(N_EXPERTS, tm) > 0.5
    val = jnp.where(emask, bsc, NEG_INF)
    eiota = lax.broadcasted_iota(I32, (N_EXPERTS, tm), 0).astype(F32)
    hits, idxs, sels = [], [], []
    for _ in range(MOE_TOPK):
        m = jnp.max(val, axis=0, keepdims=True)
        first = jnp.min(jnp.where(val == m, eiota, float(N_EXPERTS)), axis=0, keepdims=True)
        hit = eiota == first
        hits.append(hit)
        idxs.append(first.astype(I32))
        sels.append(jnp.sum(jnp.where(hit, s, 0.0), axis=0, keepdims=True))
        val = jnp.where(hit, -jnp.inf, val)
    den = sels[0]
    for k in range(1, MOE_TOPK):
        den = den + sels[k]
    onehot = hits[0].astype(F32)
    for k in range(1, MOE_TOPK):
        onehot = onehot + hits[k].astype(F32)
    before = _dot(onehot.astype(BF16), u_ref[...]) + run_ref[:, 0:1]
    pad = SUBLANES - MOE_TOPK
    zi = jnp.zeros((pad, tm), I32)
    zf = jnp.zeros((pad, tm), F32)
    ranks = [jnp.sum(jnp.where(hits[k], before, 0.0), axis=0, keepdims=True).astype(I32) for k in range(MOE_TOPK)]
    idx_ref[...] = jnp.concatenate(idxs + [zi], axis=0)
    gate_ref[...] = jnp.concatenate([sels[k] / den * ROUTED_SCALE for k in range(MOE_TOPK)] + [zf], axis=0)
    rank_ref[...] = jnp.concatenate(ranks + [zi], axis=0)
    run_ref[...] = run_ref[...] + jnp.sum(onehot, axis=1, keepdims=True)
    cnt_ref[...] = run_ref[...].astype(I32)


def route(h, w_router_t, router_bias, tm=ROUTE_TILE):
    t, d = h.shape
    assert t % tm == 0
    upper = (lax.broadcasted_iota(I32, (tm, tm), 0) < lax.broadcasted_iota(I32, (tm, tm), 1)).astype(BF16)
    tok = lambda i: (0, i)
    fixed = lambda i: (0, 0)
    return pl.pallas_call(
        functools.partial(_route_kernel, tm=tm),
        grid=(t // tm,),
        in_specs=[pl.BlockSpec((tm, d), lambda i: (i, 0)), pl.BlockSpec((N_EXPERTS, d), fixed),
                  pl.BlockSpec((N_EXPERTS, 1), fixed), pl.BlockSpec((tm, tm), fixed)],
        out_specs=[pl.BlockSpec((SUBLANES, tm), tok), pl.BlockSpec((SUBLANES, tm), tok),
                   pl.BlockSpec((SUBLANES, tm), tok), pl.BlockSpec((N_EXPERTS, LANES), fixed)],
        out_shape=[jax.ShapeDtypeStruct((SUBLANES, t), I32), jax.ShapeDtypeStruct((SUBLANES, t), F32),
                   jax.ShapeDtypeStruct((SUBLANES, t), I32), jax.ShapeDtypeStruct((N_EXPERTS, LANES), I32)],
        scratch_shapes=[pltpu.VMEM((N_EXPERTS, LANES), F32)],
        compiler_params=_params(("arbitrary",)),
        name="moe_route",
    )(h, w_router_t, router_bias.reshape(N_EXPERTS, 1), upper)


def _row_copy(src, src_row, dst, dst_row, sem):
    return pltpu.make_async_copy(src.at[pl.ds(src_row, 1)], dst.at[pl.ds(dst_row, 1)], sem)


def _dispatch_kernel(dest_ref, h_ref, xs_in, xs_out, sem, *, tb):
    del xs_in

    def start(r, carry):
        for k in range(MOE_TOPK):
            _row_copy(h_ref, r, xs_out, dest_ref[0, k, r], sem).start()
        return carry

    lax.fori_loop(0, tb, start, 0, unroll=DMA_UNROLL)
    for k in range(MOE_TOPK):
        pltpu.make_async_copy(h_ref, xs_out.at[pl.ds(0, tb)], sem).wait()


def dispatch(h, dest_tiles, n_rows, tb=SCATTER_TILE):
    t, d = h.shape
    assert t % tb == 0
    zeros = jnp.zeros((n_rows, d), F32)
    return pl.pallas_call(
        functools.partial(_dispatch_kernel, tb=tb),
        grid=(t // tb,),
        in_specs=[pl.BlockSpec((1, SUBLANES, tb), lambda i: (i, 0, 0), memory_space=pltpu.SMEM),
                  pl.BlockSpec((tb, d), lambda i: (i, 0)), pl.BlockSpec(memory_space=pl.ANY)],
        out_specs=pl.BlockSpec(memory_space=pl.ANY),
        out_shape=jax.ShapeDtypeStruct((n_rows, d), F32),
        scratch_shapes=[pltpu.SemaphoreType.DMA],
        input_output_aliases={2: 0},
        compiler_params=pltpu.CompilerParams(dimension_semantics=("arbitrary",), has_side_effects=True),
        name="moe_dispatch",
    )(dest_tiles, h, zeros)


def _col_chunks(n, width=2 * LANES):
    return [(c, min(c + width, n)) for c in range(0, n, width)]


def _expert_up_kernel(te_ref, nu_ref, x_ref, w1_ref, w3_ref, h_ref):
    i = pl.program_id(0)

    @pl.when(i < nu_ref[0])
    def _():
        x = x_ref[...].astype(BF16)
        for c0, c1 in _col_chunks(h_ref.shape[1]):
            a = _dot(x, w1_ref[0, :, c0:c1].astype(BF16))
            b = _dot(x, w3_ref[0, :, c0:c1].astype(BF16))
            h_ref[:, c0:c1] = ((a * jax.nn.sigmoid(a)) * b).astype(h_ref.dtype)

    @pl.when(i >= nu_ref[0])
    def _():
        h_ref[...] = jnp.zeros(h_ref.shape, h_ref.dtype)


def _expert_down_kernel(te_ref, nu_ref, h_ref, w2_ref, o_ref):
    i = pl.program_id(0)

    @pl.when(i < nu_ref[0])
    def _():
        h = h_ref[...]
        for c0, c1 in _col_chunks(o_ref.shape[1]):
            o_ref[:, c0:c1] = _dot(h, w2_ref[0, :, c0:c1].astype(BF16))

    @pl.when(i >= nu_ref[0])
    def _():
        o_ref[...] = jnp.zeros(o_ref.shape, o_ref.dtype)


def expert_mlp_f32(xs, tile_expert, n_used, w1, w3, w2, tm=EXPERT_TILE):
    r, d = xs.shape
    f = w1.shape[2]
    assert r % tm == 0
    nt = r // tm
    rows = lambda i, te, nu: (jnp.minimum(i, nu[0] - 1), 0)
    own = lambda i, te, nu: (i, 0)
    wsel = lambda i, te, nu: (te[jnp.minimum(i, nu[0] - 1)], 0, 0)
    big = pltpu.CompilerParams(dimension_semantics=("arbitrary",), vmem_limit_bytes=60 * 1024 * 1024)
    h = pl.pallas_call(
        _expert_up_kernel,
        grid_spec=pltpu.PrefetchScalarGridSpec(
            num_scalar_prefetch=2, grid=(nt,),
            in_specs=[pl.BlockSpec((tm, d), rows), pl.BlockSpec((1, d, f), wsel), pl.BlockSpec((1, d, f), wsel)],
            out_specs=pl.BlockSpec((tm, f), own)),
        out_shape=jax.ShapeDtypeStruct((r, f), BF16),
        compiler_params=big,
        name="expert_up",
    )(tile_expert, n_used, xs, w1, w3)
    return pl.pallas_call(
        _expert_down_kernel,
        grid_spec=pltpu.PrefetchScalarGridSpec(
            num_scalar_prefetch=2, grid=(nt,),
            in_specs=[pl.BlockSpec((tm, f), rows), pl.BlockSpec((1, f, d), wsel)],
            out_specs=pl.BlockSpec((tm, d), own)),
        out_shape=jax.ShapeDtypeStruct((r, d), F32),
        compiler_params=big,
        name="expert_down",
    )(tile_expert, n_used, h, w2)


def _expert_kernel(te_ref, nu_ref, x_ref, w1_ref, w3_ref, w2_ref, o_ref):
    i = pl.program_id(0)

    @pl.when(i < nu_ref[0])
    def _():
        x = x_ref[...].astype(BF16)
        a = _dot(x, w1_ref[0])
        h = (a * jax.nn.sigmoid(a)) * _dot(x, w3_ref[0])
        o_ref[...] = _dot(h.astype(BF16), w2_ref[0])

    @pl.when(i >= nu_ref[0])
    def _():
        o_ref[...] = jnp.zeros(o_ref.shape, o_ref.dtype)


def expert_mlp(xs, tile_expert, n_used, w1, w3, w2, tm=EXPERT_TILE):
    r, d = xs.shape
    f = w1.shape[2]
    assert r % tm == 0
    nt = r // tm
    rows = lambda i, te, nu: (jnp.minimum(i, nu[0] - 1), 0)
    wsel = lambda i, te, nu: (te[jnp.minimum(i, nu[0] - 1)], 0, 0)
    grid_spec = pltpu.PrefetchScalarGridSpec(
        num_scalar_prefetch=2,
        grid=(nt,),
        in_specs=[pl.BlockSpec((tm, d), rows), pl.BlockSpec((1, d, f), wsel),
                  pl.BlockSpec((1, d, f), wsel), pl.BlockSpec((1, f, d), wsel)],
        out_specs=pl.BlockSpec((tm, d), lambda i, te, nu: (i, 0)),
    )
    return pl.pallas_call(
        _expert_kernel,
        grid_spec=grid_spec,
        out_shape=jax.ShapeDtypeStruct((r, d), F32),
        compiler_params=_params(("arbitrary",)),
        name="expert_mlp",
    )(tile_expert, n_used, xs, w1, w3, w2)


def _combine_kernel(dest_ref, gate_ref, h_ref, sh_ref, g_ref, b_ref, o_hbm, y_ref, buf, sem, *, tb, alpha):

    def start(r, carry):
        for k in range(MOE_TOPK):
            _row_copy(o_hbm, dest_ref[0, k, r], buf.at[k], r, sem).start()
        return carry

    lax.fori_loop(0, tb, start, 0, unroll=DMA_UNROLL)
    for k in range(MOE_TOPK):
        pltpu.make_async_copy(o_hbm.at[pl.ds(0, tb)], buf.at[k], sem).wait()
    gate = gate_ref[...]
    routed = gate[:, 0:1] * buf[0]
    for k in range(1, MOE_TOPK):
        routed = routed + gate[:, k:k + 1] * buf[k]
    z = alpha * h_ref[...] + (routed + sh_ref[...])
    y_ref[...] = _layer_norm(z, g_ref[...], b_ref[...])


def combine_ln(dest_tiles, gate_t, h, shared, o_sorted, g, b, alpha, tb=COMBINE_TILE):
    t, d = h.shape
    assert t % tb == 0
    row = lambda i: (i, 0)
    fixed = lambda i: (0, 0)
    return pl.pallas_call(
        functools.partial(_combine_kernel, tb=tb, alpha=alpha),
        grid=(t // tb,),
        in_specs=[pl.BlockSpec((1, SUBLANES, tb), lambda i: (i, 0, 0), memory_space=pltpu.SMEM),
                  pl.BlockSpec((tb, SUBLANES), row), pl.BlockSpec((tb, d), row), pl.BlockSpec((tb, d), row),
                  pl.BlockSpec((1, d), fixed), pl.BlockSpec((1, d), fixed),
                  pl.BlockSpec(memory_space=pl.ANY)],
        out_specs=pl.BlockSpec((tb, d), row),
        out_shape=jax.ShapeDtypeStruct((t, d), F32),
        scratch_shapes=[pltpu.VMEM((MOE_TOPK, tb, d), F32), pltpu.SemaphoreType.DMA],
        compiler_params=_params(("arbitrary",)),
        name="moe_combine",
    )(dest_tiles, gate_t, h, shared, g, b, o_sorted)


def moe_ffn_ln(h1, w_router, router_bias, w1, w3, w2, ws1, ws3, ws2, g, b, alpha):
    t, d = h1.shape
    idx, gate, rank, cnt = route(h1, w_router.T, router_bias)
    counts = cnt[:, 0]
    padded = (counts + EXPERT_TILE - 1) // EXPERT_TILE * EXPERT_TILE
    e_ids = jnp.arange(N_EXPERTS, dtype=I32)
    pad_end = jnp.sum(jnp.where(e_ids[None, :] <= e_ids[:, None], padded[None, :], 0), axis=1)
    pad_start = pad_end - padded
    n_tiles = -(-(t * MOE_TOPK) // EXPERT_TILE) + N_EXPERTS
    tile_start = jnp.arange(n_tiles, dtype=I32) * EXPERT_TILE
    tile_expert = jnp.minimum(jnp.sum((pad_end[None, :] <= tile_start[:, None]).astype(I32), axis=1),
                              N_EXPERTS - 1)
    n_used = (pad_end[-1] // EXPERT_TILE).astype(I32).reshape(1)
    dest = jnp.sum(jnp.where(idx[:, :, None] == e_ids, pad_start, 0), axis=-1) + rank

    def tiles(a, tb):
        return a.reshape(SUBLANES, t // tb, tb).transpose(1, 0, 2)

    xs = dispatch(h1, tiles(dest, SCATTER_TILE), n_tiles * EXPERT_TILE)
    o_sorted = expert_mlp_f32(xs, tile_expert, n_used, w1, w3, w2)
    n_dense = t // ROUTE_TILE
    shared = expert_mlp_f32(h1, jnp.zeros((n_dense,), I32), jnp.full((1,), n_dense, I32),
                            ws1[None], ws3[None], ws2[None], tm=ROUTE_TILE)
    return combine_ln(tiles(dest, COMBINE_TILE), gate.T, h1, shared, o_sorted, g, b, alpha)


def _split_w_in(w_in, d_conv):
    d = w_in.shape[0]
    hd = N_HEADS * HEAD_DIM
    kvw = 2 * N_KV * HEAD_DIM
    o = 0
    parts = {}
    for name, width in (("b", d_conv), ("c", d_conv), ("h", d_conv), ("q", hd), ("cmp", kvw), ("sel", kvw),
                        ("win", kvw), ("nsa", 3 * N_HEADS), ("merge", 2 * d)):
        parts[name] = w_in[:, o:o + width]
        o += width
    nsa = parts["nsa"].reshape(d, N_KV, HPG * 3)
    parts["nsa"] = jnp.pad(nsa, ((0, 0), (0, 0), (0, LANES - HPG * 3))).reshape(d, N_KV * LANES)
    return {k: v.astype(BF16) for k, v in parts.items()}


def kernel(x_prompt, x_sample, cache_cmp_kv, cache_sel_kv, state_win_kv, state_conv, page_table,
           w_in, conv_w, w_phi_k, w_phi_v, w_conv_out, w_attn_out, w_o, ln1_g, ln1_b,
           w_router, router_bias, w_e_gate, w_e_up, w_e_down, w_s_gate, w_s_up, w_s_down,
           ln2_g, ln2_b):
    depth = w_in.shape[0]
    assert depth == 1
    alpha = (2.0 * depth) ** 0.25
    bp, tp, d = x_prompt.shape
    ns_, ts, _ = x_sample.shape
    assert ts == 1
    n_pages = page_table.shape[1]
    past_len = n_pages * PAGE_SIZE
    d_conv = conv_w.shape[2]
    kvw = 2 * N_KV * HEAD_DIM
    gw = N_KV * HEAD_DIM
    hd = N_HEADS * HEAD_DIM
    t_all = bp * tp + ns_

    w = _split_w_in(w_in[0], d_conv)
    cw = conv_w[0]
    wk_phi = w_phi_k[0].reshape(CMP_BLOCK * HEAD_DIM, HEAD_DIM).astype(BF16)
    wv_phi = w_phi_v[0].reshape(CMP_BLOCK * HEAD_DIM, HEAD_DIM).astype(BF16)
    wco = w_conv_out[0].astype(BF16)
    wao = w_attn_out[0].astype(BF16)
    wo = w_o[0].astype(BF16)
    g1, b1 = ln1_g[0].reshape(1, d), ln1_b[0].reshape(1, d)
    g2, b2 = ln2_g[0].reshape(1, d), ln2_b[0].reshape(1, d)

    xpb = x_prompt.astype(BF16)
    xp2 = xpb.reshape(bp * tp, d)
    ya_p, st_p = conv_proj(xpb, w["b"], w["c"], w["h"], cw, jnp.zeros((bp, SUBLANES, d_conv), F32))
    (q_p,) = matmul(xp2, w["q"], (BF16,))
    (cmp_p,) = matmul(xp2, w["cmp"], (F32,))
    sel_p, sel_pb = matmul(xp2, w["sel"], (F32, BF16))
    win_p, win_pb = matmul(xp2, w["win"], (F32, BF16))
    (ng_p,) = matmul(xp2, w["nsa"], (F32,))
    nc_p = tp // CMP_BLOCK
    slabs = lambda a: a.reshape(-1, HEAD_DIM)
    kc_p, vc_p = cmp_kv(slabs(cmp_p), wk_phi, wv_phi, nb=nc_p)
    pad_c = lambda a: jnp.pad(a, ((0, 0), (0, 0), (0, LANES - a.shape[2]), (0, 0))).astype(BF16)
    blk_of_key = jnp.arange(tp, dtype=I32) // SEL_BLOCK
    expand = (jnp.arange(LANES, dtype=I32)[:, None] == 2 * blk_of_key[None, :]).astype(BF16)
    o_p = prompt_attention(q_p.reshape(bp, tp, hd), ng_p.reshape(bp, tp, N_KV * LANES), pad_c(kc_p), pad_c(vc_p),
                           sel_pb.reshape(bp, tp, kvw), win_pb.reshape(bp, tp, kvw), expand)
    mix_p = merge_mixers(xp2, ya_p.reshape(bp * tp, d_conv), o_p.reshape(bp * tp, hd), w["merge"], wco, wao)
    h1 = out_proj_ln(mix_p, wo, x_prompt.reshape(bp * tp, d), g1, b1, alpha, t_all, 0,
                     prev=jnp.zeros((t_all, d), F32))

    xs2 = x_sample.reshape(ns_, d)
    xsb = xs2.astype(BF16)
    w_conv3 = jnp.concatenate([w["b"], w["c"], w["h"]], axis=1)
    (pconv_s,) = matmul(xsb, w_conv3, (F32,))
    ya_s, u_s = step_conv(pconv_s, state_conv[0, :, 0], state_conv[0, :, 1], cw)
    (q_s,) = matmul(xsb, w["q"], (BF16,))
    (cmp_s,) = matmul(xsb, w["cmp"], (F32,))
    (sel_s,) = matmul(xsb, w["sel"], (F32,))
    (win_s,) = matmul(xsb, w["win"], (F32,))
    (ng_s,) = matmul(xsb, w["nsa"], (F32,))
    n_phys = cache_cmp_kv.shape[1]
    bpp = PAGE_SIZE // CMP_BLOCK
    pages_per_tile = 32
    assert n_phys % pages_per_tile == 0
    kc_all, vc_all = cmp_kv(slabs(cache_cmp_kv), wk_phi, wv_phi, nb=pages_per_tile * bpp)

    def per_sequence(a):
        a = a.reshape(n_phys // pages_per_tile, N_KV, pages_per_tile, bpp, HEAD_DIM)
        a = a.transpose(0, 2, 3, 1, 4).reshape(n_phys, bpp, gw)
        a = a[page_table].reshape(ns_, n_pages * bpp, gw)
        return jnp.pad(a, ((0, 0), (0, LANES - n_pages * bpp), (0, 0))).astype(BF16)

    head_group = jnp.arange(LANES, dtype=I32) // HPG
    qh = jnp.pad(q_s.reshape(ns_, N_HEADS, HEAD_DIM), ((0, 0), (0, LANES - N_HEADS), (0, 0)))
    qblk = jnp.where(jnp.arange(N_KV, dtype=I32)[None, :, None, None] == head_group[None, None, None, :],
                     qh.transpose(0, 2, 1)[:, None], 0).reshape(ns_, gw, LANES).astype(BF16)
    nk = n_pages * PAGE_SIZE
    key_row = jnp.arange(nk + 2 * SUBLANES, dtype=I32)
    expand_t = ((2 * (key_row // SEL_BLOCK))[:, None] == jnp.arange(LANES, dtype=I32)[None, :]) & (key_row <= nk)[:, None]
    o_s = step_attention(qblk, ng_s.reshape(ns_, 1, N_KV * LANES), per_sequence(kc_all), per_sequence(vc_all),
                         slabs(cache_sel_kv), page_table, sel_s.reshape(ns_, 1, kvw), slabs(state_win_kv),
                         win_s.reshape(ns_, 1, kvw), expand_t.astype(BF16), past_len)
    mix_s = merge_mixers(xsb, ya_s, o_s.reshape(ns_, hd), w["merge"], wco, wao)
    h1 = out_proj_ln(mix_s, wo, xs2, g1, b1, alpha, t_all, bp * tp, prev=h1)

    y = moe_ffn_ln(h1, w_router[0], router_bias[0], w_e_gate[0], w_e_up[0], w_e_down[0],
                   w_s_gate[0], w_s_up[0], w_s_down[0], g2, b2, alpha)

    kv6 = lambda a, n, t: a.reshape(1, n, t, 2, N_KV, HEAD_DIM)
    win_keep = min(WINDOW, tp)
    new_win_s = jnp.concatenate([state_win_kv[0][:, 1:], kv6(win_s, ns_, 1)[0]], axis=1)[None]
    new_conv_s = jnp.stack([state_conv[0, :, 1], u_s], axis=1)[None]
    return (y[:bp * tp].reshape(bp, tp, d), y[bp * tp:].reshape(ns_, ts, d),
            kv6(cmp_p, bp, tp), kv6(sel_p, bp, tp), kv6(win_p, bp, tp)[:, :, tp - win_keep:],
            st_p[None, :, SUBLANES - (CONV_WIDTH - 1):],
            kv6(cmp_s, ns_, 1), kv6(sel_s, ns_, 1), new_win_s, new_conv_s)
```

```python
import functools

import jax
import jax.numpy as jnp
from jax import lax
from jax.experimental import pallas as pl
from jax.experimental.pallas import tpu as pltpu

F32 = jnp.float32
BF16 = jnp.bfloat16
I32 = jnp.int32

N_HEADS = 16
HEAD_DIM = 128
N_KV = 4
HPG = N_HEADS // N_KV
CMP_BLOCK = 32
SEL_BLOCK = 64
SEL_TOPK = 8
WINDOW = 512
PAGE_SIZE = 128
CONV_WIDTH = 3
N_EXPERTS = 64
MOE_TOPK = 6
N_GROUPS = 8
TOPK_GROUPS = 4
ROUTED_SCALE = 2.5
LN_EPS = 1e-5
NEG_INF = -1e30
FORCE_SCORE = 1e4
ATTN_SCALE = HEAD_DIM ** -0.5
LOG2_E = 1.4426950408889634

LANES = 128
SUBLANES = 8
VMEM_LIMIT = 56 * 1024 * 1024

EXPERT_TILE = 256
ROUTE_TILE = 384
COMBINE_TILE = 64
SCATTER_TILE = 128
DMA_UNROLL = 8


def _dot(a, b):
    return jnp.dot(a, b, preferred_element_type=F32)


def _dot_nt(a, b):
    return lax.dot_general(a, b, (((1,), (1,)), ((), ())), preferred_element_type=F32)


def _dot_tn(a, b):
    return lax.dot_general(a, b, (((0,), (0,)), ((), ())), preferred_element_type=F32)


def _params(sem):
    return pltpu.CompilerParams(dimension_semantics=sem, vmem_limit_bytes=VMEM_LIMIT)


def _mm_kernel(x_ref, w_ref, *o_refs):
    r = _dot(x_ref[...], w_ref[...])
    for o_ref in o_refs:
        o_ref[...] = r.astype(o_ref.dtype)


def matmul(x, w, out_dtypes, tm=1024, tn=1024):
    m, k = x.shape
    n = w.shape[1]
    tm = min(tm, m)
    tn = min(tn, n)
    assert m % tm == 0 and n % tn == 0
    outs = pl.pallas_call(
        _mm_kernel,
        grid=(n // tn, m // tm),
        in_specs=[pl.BlockSpec((tm, k), lambda j, i: (i, 0)),
                  pl.BlockSpec((k, tn), lambda j, i: (0, j))],
        out_specs=[pl.BlockSpec((tm, tn), lambda j, i: (i, j)) for _ in out_dtypes],
        out_shape=[jax.ShapeDtypeStruct((m, n), dt) for dt in out_dtypes],
        compiler_params=_params(("parallel", "parallel")),
        name="proj_mm",
    )(x, w)
    return outs


def _kv_proj_kernel(x_ref, w_ref, slab_ref, *b_refs):
    r = _dot(x_ref[...], w_ref[...])
    tm, n = r.shape
    nslab = n // HEAD_DIM
    for c in range(nslab):
        slab_ref[pl.ds(c, tm, stride=nslab), :] = r[:, c * HEAD_DIM:(c + 1) * HEAD_DIM]
    for b_ref in b_refs:
        b_ref[...] = r.astype(b_ref.dtype)


def kv_proj(x, w, with_bf16, tm=1024):
    m, k = x.shape
    n = w.shape[1]
    nslab = n // HEAD_DIM
    tm = min(tm, m)
    assert m % tm == 0
    out_specs = [pl.BlockSpec((tm * nslab, HEAD_DIM), lambda i: (i, 0))]
    out_shape = [jax.ShapeDtypeStruct((m * nslab, HEAD_DIM), F32)]
    if with_bf16:
        out_specs.append(pl.BlockSpec((tm, n), lambda i: (i, 0)))
        out_shape.append(jax.ShapeDtypeStruct((m, n), BF16))
    return pl.pallas_call(
        _kv_proj_kernel,
        grid=(m // tm,),
        in_specs=[pl.BlockSpec((tm, k), lambda i: (i, 0)), pl.BlockSpec((k, n), lambda i: (0, 0))],
        out_specs=out_specs,
        out_shape=out_shape,
        compiler_params=_params(("parallel",)),
        name="kv_proj",
    )(x, w)


def _conv_proj_kernel(x_ref, wb_ref, wc_ref, wh_ref, cw_ref, pre_ref, ya_ref, st_ref, s_ref, *, tm):
    i = pl.program_id(2)

    @pl.when(i == 0)
    def _():
        s_ref[0:SUBLANES, :] = pre_ref[0]

    x = x_ref[0]
    pb = _dot(x, wb_ref[...])
    u = _dot(x, wc_ref[...]) * _dot(x, wh_ref[...])
    s_ref[SUBLANES:SUBLANES + tm, :] = u
    u1 = s_ref[SUBLANES - 1:SUBLANES - 1 + tm, :]
    u2 = s_ref[SUBLANES - 2:SUBLANES - 2 + tm, :]
    cw = cw_ref[...]
    y = cw[0:1] * u2 + cw[1:2] * u1 + cw[2:3] * u
    ya_ref[0] = (pb * y).astype(ya_ref.dtype)
    last = s_ref[tm:tm + SUBLANES, :]
    s_ref[0:SUBLANES, :] = last
    st_ref[0] = last


def conv_proj(xb, wb, wc, wh, conv_w, prefix8, tm=512, tn=512):
    b, t, d = xb.shape
    dc = wb.shape[1]
    tm = min(tm, t)
    assert t % tm == 0 and dc % tn == 0 and tm >= SUBLANES
    return pl.pallas_call(
        functools.partial(_conv_proj_kernel, tm=tm),
        grid=(dc // tn, b, t // tm),
        in_specs=[pl.BlockSpec((1, tm, d), lambda j, bb, i: (bb, i, 0)),
                  pl.BlockSpec((d, tn), lambda j, bb, i: (0, j)),
                  pl.BlockSpec((d, tn), lambda j, bb, i: (0, j)),
                  pl.BlockSpec((d, tn), lambda j, bb, i: (0, j)),
                  pl.BlockSpec((CONV_WIDTH, tn), lambda j, bb, i: (0, j)),
                  pl.BlockSpec((1, SUBLANES, tn), lambda j, bb, i: (bb, 0, j))],
        out_specs=[pl.BlockSpec((1, tm, tn), lambda j, bb, i: (bb, i, j)),
                   pl.BlockSpec((1, SUBLANES, tn), lambda j, bb, i: (bb, 0, j))],
        out_shape=[jax.ShapeDtypeStruct((b, t, dc), BF16),
                   jax.ShapeDtypeStruct((b, SUBLANES, dc), F32)],
        scratch_shapes=[pltpu.VMEM((tm + SUBLANES, tn), F32)],
        compiler_params=_params(("parallel", "parallel", "arbitrary")),
        name="conv_proj",
    )(xb, wb, wc, wh, conv_w, prefix8)


def _step_conv_kernel(p_ref, s0_ref, s1_ref, cw_ref, ya_ref, u_ref, *, dc):
    pb = p_ref[:, 0:dc]
    u = p_ref[:, dc:2 * dc] * p_ref[:, 2 * dc:3 * dc]
    cw = cw_ref[...]
    y = cw[0:1] * s0_ref[...] + cw[1:2] * s1_ref[...] + cw[2:3] * u
    ya_ref[...] = (pb * y).astype(ya_ref.dtype)
    u_ref[...] = u


def step_conv(p, s0, s1, conv_w):
    n, dc3 = p.shape
    dc = dc3 // 3
    return pl.pallas_call(
        functools.partial(_step_conv_kernel, dc=dc),
        out_shape=[jax.ShapeDtypeStruct((n, dc), BF16), jax.ShapeDtypeStruct((n, dc), F32)],
        compiler_params=pltpu.CompilerParams(vmem_limit_bytes=VMEM_LIMIT),
        name="step_conv",
    )(p, s0, s1, conv_w)


def _cmp_kv_kernel(x_ref, wk_ref, wv_ref, ok_ref, ov_ref, *, nb):
    nslab = 2 * N_KV

    def rows(l, c):
        return x_ref[pl.ds(l * nslab + c, nb, stride=CMP_BLOCK * nslab), :].astype(BF16)

    for kv, (w_ref, o_ref) in enumerate(((wk_ref, ok_ref), (wv_ref, ov_ref))):
        per_group = [jnp.concatenate([rows(l, kv * N_KV + g) for l in range(CMP_BLOCK)], axis=1)
                     for g in range(N_KV)]
        lhs = jnp.concatenate(per_group, axis=0)
        r = _dot(lhs, w_ref[...])
        for g in range(N_KV):
            o_ref[0, g] = r[g * nb:(g + 1) * nb]


def cmp_kv(raw, wk, wv, nb):
    n_rows, width = raw.shape
    rows_per_step = nb * CMP_BLOCK * 2 * N_KV
    assert n_rows % rows_per_step == 0 and width == HEAD_DIM
    nt = n_rows // rows_per_step
    return pl.pallas_call(
        functools.partial(_cmp_kv_kernel, nb=nb),
        grid=(nt,),
        in_specs=[pl.BlockSpec((rows_per_step, width), lambda i: (i, 0)),
                  pl.BlockSpec(wk.shape, lambda i: (0, 0)),
                  pl.BlockSpec(wv.shape, lambda i: (0, 0))],
        out_specs=[pl.BlockSpec((1, N_KV, nb, HEAD_DIM), lambda i: (i, 0, 0, 0))] * 2,
        out_shape=[jax.ShapeDtypeStruct((nt, N_KV, nb, HEAD_DIM), F32)] * 2,
        compiler_params=_params(("parallel",)),
        name="cmp_kv",
    )(raw, wk, wv)


def _top_k_mask(val, iota, axis, k, size):
    sel = jnp.zeros(val.shape, F32)
    iota = iota.astype(F32)
    for _ in range(k):
        m = jnp.max(val, axis=axis, keepdims=True)
        first = jnp.min(jnp.where(val == m, iota, float(size)), axis=axis, keepdims=True)
        hit = iota == first
        sel = jnp.where(hit, 1.0, sel)
        val = jnp.where(hit, -jnp.inf, val)
    return sel


def _prompt_attn_kernel(q_ref, ng_ref, kc_ref, vc_ref, ks_ref, vs_ref, kw_ref, vw_ref, e_ref, o_ref,
                        m_ref, l_ref, acc_ref, oacc_ref, *, tq, tk, nc, ns):
    qt = pl.program_id(2)
    t0 = qt * tq
    qpos = t0 + lax.broadcasted_iota(I32, (tq, 1), 0)
    gates = jax.nn.sigmoid(ng_ref[0])

    qcol = t0 + lax.broadcasted_iota(I32, (1, tq), 1)
    rowb = lax.broadcasted_iota(I32, (LANES, tq), 0)
    vis = ((rowb + 1) * CMP_BLOCK - 1 <= qcol) & (rowb < nc)
    visf = vis.astype(F32)
    kc = kc_ref[0, 0]
    vc = vc_ref[0, 0]
    imp = jnp.zeros((LANES, tq), F32)
    for h in range(HPG):
        qh = q_ref[0, :, h * HEAD_DIM:(h + 1) * HEAD_DIM]
        s = jnp.where(vis, _dot_nt(kc, qh) * ATTN_SCALE, NEG_INF)
        e = jnp.exp(s - jnp.max(s, axis=0, keepdims=True))
        p = e / jnp.sum(e, axis=0, keepdims=True) * visf
        imp = imp + p
        oacc_ref[h] = gates[:, h * 3:h * 3 + 1] * _dot_tn(p.astype(BF16), vc)

    pair = imp + pltpu.roll(imp, LANES - 1, axis=0)
    blk = rowb >> 1
    cand = ((rowb & 1) == 0) & (blk < ns)
    cur = qcol >> (SEL_BLOCK.bit_length() - 1)
    forced = (blk == 0) | (blk == cur) | (blk == cur - 1)
    val = jnp.where(blk * SEL_BLOCK <= qcol, jnp.where(forced, FORCE_SCORE, pair), -1.0)
    val = jnp.where(cand, val, -jnp.inf)
    selb = _top_k_mask(val, rowb, 0, SEL_TOPK, LANES).astype(BF16)

    def flash(k_ref, v_ref, kt_lo, kt_hi, mask_fn, branch):
        m_ref[...] = jnp.full(m_ref.shape, NEG_INF, F32)
        l_ref[...] = jnp.zeros(l_ref.shape, F32)
        acc_ref[...] = jnp.zeros(acc_ref.shape, F32)

        def body(kt, carry):
            k0 = pl.multiple_of(kt * tk, tk)
            k = k_ref[0, pl.ds(k0, tk), :]
            v = v_ref[0, pl.ds(k0, tk), :]
            mask = mask_fn(k0)
            for h in range(HPG):
                qh = q_ref[0, :, h * HEAD_DIM:(h + 1) * HEAD_DIM]
                s = jnp.where(mask, _dot_nt(qh, k) * (ATTN_SCALE * LOG2_E), NEG_INF)
                m_prev = m_ref[h]
                m_next = jnp.maximum(m_prev, jnp.max(s, axis=-1, keepdims=True))
                alpha = jnp.exp2(m_prev - m_next)
                p = jnp.exp2(s - jnp.tile(m_next, (1, tk // LANES)))
                psum = p[:, 0:LANES]
                for c in range(1, tk // LANES):
                    psum = psum + p[:, c * LANES:(c + 1) * LANES]
                l_ref[h] = alpha * l_ref[h] + psum
                acc_ref[h] = alpha * acc_ref[h] + _dot(p.astype(BF16), v)
                m_ref[h] = m_next
            return carry

        lax.fori_loop(kt_lo, kt_hi, body, 0)
        for h in range(HPG):
            c = h * 3 + branch
            l_tot = jnp.sum(l_ref[h], axis=-1, keepdims=True)
            oacc_ref[h] = oacc_ref[h] + gates[:, c:c + 1] * (acc_ref[h] / l_tot)

    kt_hi = (t0 + tq - 1) // tk + 1

    def sel_mask(k0):
        kpos = k0 + lax.broadcasted_iota(I32, (1, tk), 1)
        chosen = _dot_tn(selb, e_ref[:, pl.ds(k0, tk)]) > 0.5
        return chosen & (kpos <= qpos)

    def win_mask(k0):
        kpos = k0 + lax.broadcasted_iota(I32, (1, tk), 1)
        return (kpos <= qpos) & (kpos >= qpos - WINDOW)

    flash(ks_ref, vs_ref, 0, kt_hi, sel_mask, 1)
    flash(kw_ref, vw_ref, jnp.maximum(t0 - WINDOW, 0) // tk, kt_hi, win_mask, 2)
    for h in range(HPG):
        o_ref[0, :, h * HEAD_DIM:(h + 1) * HEAD_DIM] = oacc_ref[h].astype(o_ref.dtype)


def prompt_attention(q, ng, kc, vc, sel_b, win_b, expand, tq=256, tk=256):
    b, t, _ = q.shape
    assert t % tq == 0 and t % tk == 0 and tq % tk == 0
    nc = t // CMP_BLOCK
    ns = max(-(-t // SEL_BLOCK), SEL_TOPK)
    assert nc <= LANES and 2 * ns <= LANES and ns * SEL_BLOCK == t
    gw = HPG * HEAD_DIM
    kern = functools.partial(_prompt_attn_kernel, tq=tq, tk=tk, nc=nc, ns=ns)
    kv_spec = lambda off: pl.BlockSpec((1, t, HEAD_DIM), lambda bb, g, i: (bb, 0, off + g))
    return pl.pallas_call(
        kern,
        grid=(b, N_KV, t // tq),
        in_specs=[pl.BlockSpec((1, tq, gw), lambda bb, g, i: (bb, i, g)),
                  pl.BlockSpec((1, tq, LANES), lambda bb, g, i: (bb, i, g)),
                  pl.BlockSpec((1, 1, LANES, HEAD_DIM), lambda bb, g, i: (bb, g, 0, 0)),
                  pl.BlockSpec((1, 1, LANES, HEAD_DIM), lambda bb, g, i: (bb, g, 0, 0)),
                  kv_spec(0), kv_spec(N_KV), kv_spec(0), kv_spec(N_KV),
                  pl.BlockSpec((LANES, t), lambda bb, g, i: (0, 0))],
        out_specs=pl.BlockSpec((1, tq, gw), lambda bb, g, i: (bb, i, g)),
        out_shape=jax.ShapeDtypeStruct(q.shape, BF16),
        scratch_shapes=[pltpu.VMEM((HPG, tq, LANES), F32), pltpu.VMEM((HPG, tq, LANES), F32),
                        pltpu.VMEM((HPG, tq, HEAD_DIM), F32), pltpu.VMEM((HPG, tq, HEAD_DIM), F32)],
        compiler_params=_params(("parallel", "parallel", "arbitrary")),
        name="prompt_attn",
    )(q, ng, kc, vc, sel_b, sel_b, win_b, win_b, expand)


def _step_attn_kernel(*refs, n_pages, past_len, nc, ns):
    (_, qb_ref, ng_ref, kc_ref, vc_ref) = refs[:5]
    pages = refs[5:5 + n_pages]
    (snew_ref, win_ref, wnew_ref, wslab_ref, et_ref, o_ref, wout_ref, kbuf, vbuf, kwbuf, vwbuf) = refs[5 + n_pages:]
    gw = N_KV * HEAD_DIM
    qb = qb_ref[0]
    lane = lax.broadcasted_iota(I32, (LANES, LANES), 1)
    rowi = lax.broadcasted_iota(I32, (LANES, LANES), 0)

    def attend(k, v, mask):
        s = jnp.where(mask, _dot(k, qb) * ATTN_SCALE, NEG_INF)
        e = jnp.exp(s - jnp.max(s, axis=0, keepdims=True))
        p = e / jnp.sum(e, axis=0, keepdims=True) * mask.astype(F32)
        return p, _dot_tn(p.astype(BF16), v)

    vis = ((rowi + 1) * CMP_BLOCK - 1 <= past_len) & (rowi < nc)
    pc, oc = attend(kc_ref[0], vc_ref[0], vis)

    a = pc + pltpu.roll(pc, LANES - 1, axis=1)
    a = a + pltpu.roll(a, LANES - 2, axis=1)
    pair = a + pltpu.roll(a, LANES - 1, axis=0)
    blk = rowi >> 1
    cand = ((rowi & 1) == 0) & (blk < ns)
    cur = past_len // SEL_BLOCK
    forced = (blk == 0) | (blk == cur) | (blk == cur - 1)
    val = jnp.where(blk * SEL_BLOCK <= past_len, jnp.where(forced, FORCE_SCORE, pair), -1.0)
    val = jnp.where(cand, val, -jnp.inf)
    sel = _top_k_mask(val, rowi, 0, SEL_TOPK, LANES)
    sel = jnp.where(((lane & (HPG - 1)) == 0) & (lane < N_HEADS), sel, 0.0)
    selh = sel
    for r in range(1, HPG):
        selh = selh + pltpu.roll(sel, r, axis=1)

    nk = n_pages * PAGE_SIZE
    nslab = 2 * N_KV
    tail = lax.broadcasted_iota(I32, (2 * SUBLANES, gw), 0) == 0
    for p in range(n_pages):
        for g in range(N_KV):
            cols = slice(g * HEAD_DIM, (g + 1) * HEAD_DIM)
            kbuf[p * PAGE_SIZE:(p + 1) * PAGE_SIZE, cols] = pages[p][pl.ds(g, PAGE_SIZE, stride=nslab), :].astype(BF16)
            vbuf[p * PAGE_SIZE:(p + 1) * PAGE_SIZE, cols] = pages[p][pl.ds(N_KV + g, PAGE_SIZE, stride=nslab), :].astype(BF16)
    kbuf[nk:nk + 2 * SUBLANES, :] = jnp.where(tail, snew_ref[0, :, 0:gw], 0.0).astype(BF16)
    vbuf[nk:nk + 2 * SUBLANES, :] = jnp.where(tail, snew_ref[0, :, gw:2 * gw], 0.0).astype(BF16)
    chosen = _dot(et_ref[...], selh.astype(BF16)) > 0.5
    _, osel = attend(kbuf[...], vbuf[...], chosen)

    nw = win_ref.shape[0] // nslab
    for g in range(N_KV):
        cols = slice(g * HEAD_DIM, (g + 1) * HEAD_DIM)
        kwbuf[0:nw, cols] = win_ref[pl.ds(g, nw, stride=nslab), :].astype(BF16)
        vwbuf[0:nw, cols] = win_ref[pl.ds(N_KV + g, nw, stride=nslab), :].astype(BF16)
    kwbuf[nw:nw + 2 * SUBLANES, :] = jnp.where(tail, wnew_ref[0, :, 0:gw], 0.0).astype(BF16)
    vwbuf[nw:nw + 2 * SUBLANES, :] = jnp.where(tail, wnew_ref[0, :, gw:2 * gw], 0.0).astype(BF16)
    wrow = lax.broadcasted_iota(I32, (nw + 2 * SUBLANES, LANES), 0)
    _, owin = attend(kwbuf[...], vwbuf[...], wrow <= nw)
    wout_ref[0:(nw - 1) * nslab, :] = win_ref[nslab:nw * nslab, :]
    wout_ref[(nw - 1) * nslab:nw * nslab, :] = wslab_ref[...]

    gates = jax.nn.sigmoid(ng_ref[0])
    for h in range(N_HEADS):
        g, hh = divmod(h, HPG)
        c = g * LANES + hh * 3
        cols = slice(g * HEAD_DIM, (g + 1) * HEAD_DIM)
        o_h = (gates[:, c:c + 1] * oc[h:h + 1, cols] + gates[:, c + 1:c + 2] * osel[h:h + 1, cols]
               + gates[:, c + 2:c + 3] * owin[h:h + 1, cols])
        o_ref[0, :, h * HEAD_DIM:(h + 1) * HEAD_DIM] = o_h.astype(o_ref.dtype)


def step_attention(qblk, ng, kc, vc, sel_pool, page_table, sel_new, win_state, win_new, expand_t, past_len):
    n, n_pages = page_table.shape
    gw = N_KV * HEAD_DIM
    nslab = 2 * N_KV
    nk = n_pages * PAGE_SIZE
    nw = win_state.shape[0] // (n * nslab)
    nc = (past_len + 1) // CMP_BLOCK
    ns = max(-(-(past_len + 1) // SEL_BLOCK), SEL_TOPK)
    assert nc <= LANES and 2 * ns <= LANES and nw == WINDOW and past_len >= WINDOW
    kern = functools.partial(_step_attn_kernel, n_pages=n_pages, past_len=past_len, nc=nc, ns=ns)
    page_specs = [pl.BlockSpec((PAGE_SIZE * nslab, HEAD_DIM), functools.partial(lambda i, pt, p: (pt[i, p], 0), p=p))
                  for p in range(n_pages)]
    grid_spec = pltpu.PrefetchScalarGridSpec(
        num_scalar_prefetch=1,
        grid=(n,),
        in_specs=[pl.BlockSpec((1, gw, LANES), lambda i, pt: (i, 0, 0)),
                  pl.BlockSpec((1, 1, N_KV * LANES), lambda i, pt: (i, 0, 0)),
                  pl.BlockSpec((1, LANES, gw), lambda i, pt: (i, 0, 0)),
                  pl.BlockSpec((1, LANES, gw), lambda i, pt: (i, 0, 0))]
        + page_specs
        + [pl.BlockSpec((1, 1, 2 * gw), lambda i, pt: (i, 0, 0)),
           pl.BlockSpec((nw * nslab, HEAD_DIM), lambda i, pt: (i, 0)),
           pl.BlockSpec((1, 1, 2 * gw), lambda i, pt: (i, 0, 0)),
           pl.BlockSpec((nslab, HEAD_DIM), lambda i, pt: (i, 0)),
           pl.BlockSpec((nk + 2 * SUBLANES, LANES), lambda i, pt: (0, 0))],
        out_specs=[pl.BlockSpec((1, 1, N_HEADS * HEAD_DIM), lambda i, pt: (i, 0, 0)),
                   pl.BlockSpec((nw * nslab, HEAD_DIM), lambda i, pt: (i, 0))],
        scratch_shapes=[pltpu.VMEM((nk + 2 * SUBLANES, gw), BF16), pltpu.VMEM((nk + 2 * SUBLANES, gw), BF16),
                        pltpu.VMEM((nw + 2 * SUBLANES, gw), BF16), pltpu.VMEM((nw + 2 * SUBLANES, gw), BF16)],
    )
    return pl.pallas_call(
        kern,
        grid_spec=grid_spec,
        out_shape=[jax.ShapeDtypeStruct((n, 1, N_HEADS * HEAD_DIM), BF16),
                   jax.ShapeDtypeStruct(win_state.shape, F32)],
        compiler_params=_params(("arbitrary",)),
        name="step_attn",
    )(page_table, qblk, ng, kc, vc, *([sel_pool] * n_pages), sel_new, win_state, win_new,
      win_new.reshape(n * nslab, HEAD_DIM), expand_t)


def _merge_kernel(x_ref, ya_ref, ob_ref, wga_ref, wgb_ref, wco_ref, wao_ref, o_ref):
    x = x_ref[...]
    ga = jax.nn.sigmoid(_dot(x, wga_ref[...]))
    gb = jax.nn.sigmoid(_dot(x, wgb_ref[...]))
    y_a = _dot(ya_ref[...], wco_ref[...])
    y_b = _dot(ob_ref[...], wao_ref[...])
    o_ref[...] = (ga * y_a + gb * y_b).astype(o_ref.dtype)


def merge_mixers(xb, ya_in, ob, w_merge, w_conv_out, w_attn_out, tm=512, tn=512):
    m, d = xb.shape
    tm = min(tm, m)
    assert m % tm == 0 and d % tn == 0
    nj = d // tn
    row = lambda j, i: (i, 0)
    col = lambda j, i: (0, j)
    return pl.pallas_call(
        _merge_kernel,
        grid=(nj, m // tm),
        in_specs=[pl.BlockSpec((tm, d), row), pl.BlockSpec((tm, ya_in.shape[1]), row),
                  pl.BlockSpec((tm, ob.shape[1]), row),
                  pl.BlockSpec((d, tn), col), pl.BlockSpec((d, tn), lambda j, i: (0, j + nj)),
                  pl.BlockSpec((w_conv_out.shape[0], tn), col), pl.BlockSpec((w_attn_out.shape[0], tn), col)],
        out_specs=pl.BlockSpec((tm, tn), lambda j, i: (i, j)),
        out_shape=jax.ShapeDtypeStruct((m, d), BF16),
        compiler_params=_params(("parallel", "parallel")),
        name="merge_mixers",
    )(xb, ya_in, ob, w_merge, w_merge, w_conv_out, w_attn_out)


def _layer_norm(z, g, b):
    mu = jnp.mean(z, axis=-1, keepdims=True)
    zc = z - mu
    var = jnp.mean(zc * zc, axis=-1, keepdims=True)
    return zc * lax.rsqrt(var + LN_EPS) * g + b


def _out_proj_ln_kernel(m_ref, w_ref, x_ref, g_ref, b_ref, *rest, alpha):
    o_ref = rest[-1]
    z = alpha * x_ref[...] + _dot(m_ref[...], w_ref[...])
    o_ref[...] = _layer_norm(z, g_ref[...], b_ref[...])


def out_proj_ln(mixin, w_o, x, g, b, alpha, total_rows, row_offset, prev=None, tm=512):
    m, d = x.shape
    tm = min(tm, m)
    assert m % tm == 0 and row_offset % tm == 0
    off = row_offset // tm
    row = lambda i: (i, 0)
    fixed = lambda i: (0, 0)
    in_specs = [pl.BlockSpec((tm, d), row), pl.BlockSpec((d, d), fixed), pl.BlockSpec((tm, d), row),
                pl.BlockSpec((1, d), fixed), pl.BlockSpec((1, d), fixed)]
    args = [mixin, w_o, x, g, b]
    aliases = {}
    if prev is not None:
        in_specs.append(pl.BlockSpec(memory_space=pl.ANY))
        args.append(prev)
        aliases = {5: 0}
    return pl.pallas_call(
        functools.partial(_out_proj_ln_kernel, alpha=alpha),
        grid=(m // tm,),
        in_specs=in_specs,
        out_specs=pl.BlockSpec((tm, d), lambda i: (i + off, 0)),
        out_shape=jax.ShapeDtypeStruct((total_rows, d), F32),
        input_output_aliases=aliases,
        compiler_params=_params(("parallel",)),
        name="out_proj_ln",
    )(*args)


def _route_kernel(h_ref, wr_ref, rb_ref, u_ref, idx_ref, gate_ref, rank_ref, cnt_ref, run_ref, *, tm):
    i = pl.program_id(0)

    @pl.when(i == 0)
    def _():
        run_ref[...] = jnp.zeros(run_ref.shape, F32)

    logits = lax.dot_general(wr_ref[...], h_ref[...], (((1,), (1,)), ((), ())),
                             precision=lax.Precision.HIGHEST, preferred_element_type=F32)
    s = jax.nn.sigmoid(logits)
    bsc = s + rb_ref[...]
    gsz = N_EXPERTS // N_GROUPS
    x3 = bsc.reshape(N_GROUPS, gsz, tm)
    sub = lax.broadcasted_iota(I32, (N_GROUPS, gsz, tm), 1).astype(F32)
    m1 = jnp.max(x3, axis=1, keepdims=True)
    i1 = jnp.min(jnp.where(x3 == m1, sub, float(gsz)), axis=1, keepdims=True)
    m2 = jnp.max(jnp.where(sub == i1, -jnp.inf, x3), axis=1, keepdims=True)
    gsc = (m1 + m2).reshape(N_GROUPS, tm)
    giota = lax.broadcasted_iota(I32, (N_GROUPS, tm), 0)
    gsel = _top_k_mask(gsc, giota, 0, TOPK_GROUPS, N_GROUPS)
    emask = jnp.broadcast_to(gsel.reshape(N_GROUPS, 1, tm), (N_GROUPS, gsz, tm)).reshape(N_EXPERTS, tm) > 0.5
    val = jnp.where(emask, bsc, NEG_INF)
    eiota = lax.broadcasted_iota(I32, (N_EXPERTS, tm), 0).astype(F32)
    hits, idxs, sels = [], [], []
    for _ in range(MOE_TOPK):
        m = jnp.max(val, axis=0, keepdims=True)
        first = jnp.min(jnp.where(val == m, eiota, float(N_EXPERTS)), axis=0, keepdims=True)
        hit = eiota == first
        hits.append(hit)
        idxs.append(first.astype(I32))
        sels.append(jnp.sum(jnp.where(hit, s, 0.0), axis=0, keepdims=True))
        val = jnp.where(hit, -jnp.inf, val)
    den = sels[0]
    for k in range(1, MOE_TOPK):
        den = den + sels[k]
    onehot = hits[0].astype(F32)
    for k in range(1, MOE_TOPK):
        onehot = onehot + hits[k].astype(F32)
    before = _dot(onehot.astype(BF16), u_ref[...]) + run_ref[:, 0:1]
    pad = SUBLANES - MOE_TOPK
    zi = jnp.zeros((pad, tm), I32)
    zf = jnp.zeros((pad, tm), F32)
    ranks = [jnp.sum(jnp.where(hits[k], before, 0.0), axis=0, keepdims=True).astype(I32) for k in range(MOE_TOPK)]
    idx_ref[...] = jnp.concatenate(idxs + [zi], axis=0)
    gate_ref[...] = jnp.concatenate([sels[k] / den * ROUTED_SCALE for k in range(MOE_TOPK)] + [zf], axis=0)
    rank_ref[...] = jnp.concatenate(ranks + [zi], axis=0)
    run_ref[...] = run_ref[...] + jnp.sum(onehot, axis=1, keepdims=True)
    cnt_ref[...] = run_ref[...].astype(I32)


def route(h, w_router_t, router_bias, tm=ROUTE_TILE):
    t, d = h.shape
    assert t % tm == 0
    upper = (lax.broadcasted_iota(I32, (tm, tm), 0) < lax.broadcasted_iota(I32, (tm, tm), 1)).astype(BF16)
    tok = lambda i: (0, i)
    fixed = lambda i: (0, 0)
    return pl.pallas_call(
        functools.partial(_route_kernel, tm=tm),
        grid=(t // tm,),
        in_specs=[pl.BlockSpec((tm, d), lambda i: (i, 0)), pl.BlockSpec((N_EXPERTS, d), fixed),
                  pl.BlockSpec((N_EXPERTS, 1), fixed), pl.BlockSpec((tm, tm), fixed)],
        out_specs=[pl.BlockSpec((SUBLANES, tm), tok), pl.BlockSpec((SUBLANES, tm), tok),
                   pl.BlockSpec((SUBLANES, tm), tok), pl.BlockSpec((N_EXPERTS, LANES), fixed)],
        out_shape=[jax.ShapeDtypeStruct((SUBLANES, t), I32), jax.ShapeDtypeStruct((SUBLANES, t), F32),
                   jax.ShapeDtypeStruct((SUBLANES, t), I32), jax.ShapeDtypeStruct((N_EXPERTS, LANES), I32)],
        scratch_shapes=[pltpu.VMEM((N_EXPERTS, LANES), F32)],
        compiler_params=_params(("arbitrary",)),
        name="moe_route",
    )(h, w_router_t, router_bias.reshape(N_EXPERTS, 1), upper)


def _row_copy(src, src_row, dst, dst_row, sem):
    return pltpu.make_async_copy(src.at[pl.ds(src_row, 1)], dst.at[pl.ds(dst_row, 1)], sem)


def _dispatch_kernel(pe_ref, dest_ref, h_ref, xs_out, zbuf, sem, zsem, *, tb):

    @pl.when(pl.program_id(0) == 0)
    def _():
        zbuf[...] = jnp.zeros(zbuf.shape, zbuf.dtype)

        def nonempty(e):
            return pe_ref[e] > jnp.where(e > 0, pe_ref[jnp.maximum(e - 1, 0)], 0)

        def zero_tile(e):
            first = pl.multiple_of(pe_ref[e] - EXPERT_TILE, EXPERT_TILE)
            return pltpu.make_async_copy(zbuf, xs_out.at[pl.ds(first, EXPERT_TILE)], zsem)

        def zstart(e, carry):
            @pl.when(nonempty(e))
            def _():
                zero_tile(e).start()
            return carry

        def zwait(e, carry):
            @pl.when(nonempty(e))
            def _():
                zero_tile(e).wait()
            return carry

        lax.fori_loop(0, N_EXPERTS, zstart, 0)
        lax.fori_loop(0, N_EXPERTS, zwait, 0)

    def start(r, carry):
        for k in range(MOE_TOPK):
            _row_copy(h_ref, r, xs_out, dest_ref[0, k, r], sem).start()
        return carry

    lax.fori_loop(0, tb, start, 0, unroll=DMA_UNROLL)
    for k in range(MOE_TOPK):
        pltpu.make_async_copy(h_ref, xs_out.at[pl.ds(0, tb)], sem).wait()


def dispatch(h, dest_tiles, pad_end, n_rows, tb=SCATTER_TILE):
    t, d = h.shape
    assert t % tb == 0
    grid_spec = pltpu.PrefetchScalarGridSpec(
        num_scalar_prefetch=1,
        grid=(t // tb,),
        in_specs=[pl.BlockSpec((1, SUBLANES, tb), lambda i, pe: (i, 0, 0), memory_space=pltpu.SMEM),
                  pl.BlockSpec((tb, d), lambda i, pe: (i, 0))],
        out_specs=pl.BlockSpec(memory_space=pl.ANY),
        scratch_shapes=[pltpu.VMEM((EXPERT_TILE, d), F32), pltpu.SemaphoreType.DMA, pltpu.SemaphoreType.DMA],
    )
    return pl.pallas_call(
        functools.partial(_dispatch_kernel, tb=tb),
        grid_spec=grid_spec,
        out_shape=jax.ShapeDtypeStruct((n_rows, d), F32),
        compiler_params=pltpu.CompilerParams(dimension_semantics=("arbitrary",), has_side_effects=True),
        name="moe_dispatch",
    )(pad_end, dest_tiles, h)


def _col_chunks(n, width=2 * LANES):
    return [(c, min(c + width, n)) for c in range(0, n, width)]


def _expert_up_kernel(te_ref, nu_ref, x_ref, w1_ref, w3_ref, h_ref):
    i = pl.program_id(0)

    @pl.when(i < nu_ref[0])
    def _():
        x = x_ref[...].astype(BF16)
        for c0, c1 in _col_chunks(h_ref.shape[1]):
            a = _dot(x, w1_ref[0, :, c0:c1].astype(BF16))
            b = _dot(x, w3_ref[0, :, c0:c1].astype(BF16))
            h_ref[:, c0:c1] = ((a * jax.nn.sigmoid(a)) * b).astype(h_ref.dtype)

    @pl.when(i >= nu_ref[0])
    def _():
        h_ref[...] = jnp.zeros(h_ref.shape, h_ref.dtype)


def _expert_down_kernel(te_ref, nu_ref, h_ref, w2_ref, o_ref):
    i = pl.program_id(0)

    @pl.when(i < nu_ref[0])
    def _():
        h = h_ref[...]
        for c0, c1 in _col_chunks(o_ref.shape[1]):
            o_ref[:, c0:c1] = _dot(h, w2_ref[0, :, c0:c1].astype(BF16))

    @pl.when(i >= nu_ref[0])
    def _():
        o_ref[...] = jnp.zeros(o_ref.shape, o_ref.dtype)


def expert_mlp_f32(xs, tile_expert, n_used, w1, w3, w2, tm=EXPERT_TILE):
    r, d = xs.shape
    f = w1.shape[2]
    assert r % tm == 0
    nt = r // tm
    rows = lambda i, te, nu: (jnp.minimum(i, nu[0] - 1), 0)
    own = lambda i, te, nu: (i, 0)
    wsel = lambda i, te, nu: (te[jnp.minimum(i, nu[0] - 1)], 0, 0)
    big = pltpu.CompilerParams(dimension_semantics=("arbitrary",), vmem_limit_bytes=60 * 1024 * 1024)
    h = pl.pallas_call(
        _expert_up_kernel,
        grid_spec=pltpu.PrefetchScalarGridSpec(
            num_scalar_prefetch=2, grid=(nt,),
            in_specs=[pl.BlockSpec((tm, d), rows), pl.BlockSpec((1, d, f), wsel), pl.BlockSpec((1, d, f), wsel)],
            out_specs=pl.BlockSpec((tm, f), own)),
        out_shape=jax.ShapeDtypeStruct((r, f), BF16),
        compiler_params=big,
        name="expert_up",
    )(tile_expert, n_used, xs, w1, w3)
    return pl.pallas_call(
        _expert_down_kernel,
        grid_spec=pltpu.PrefetchScalarGridSpec(
            num_scalar_prefetch=2, grid=(nt,),
            in_specs=[pl.BlockSpec((tm, f), rows), pl.BlockSpec((1, f, d), wsel)],
            out_specs=pl.BlockSpec((tm, d), own)),
        out_shape=jax.ShapeDtypeStruct((r, d), F32),
        compiler_params=big,
        name="expert_down",
    )(tile_expert, n_used, h, w2)


def _expert_kernel(te_ref, nu_ref, x_ref, w1_ref, w3_ref, w2_ref, o_ref):
    i = pl.program_id(0)

    @pl.when(i < nu_ref[0])
    def _():
        x = x_ref[...].astype(BF16)
        a = _dot(x, w1_ref[0])
        h = (a * jax.nn.sigmoid(a)) * _dot(x, w3_ref[0])
        o_ref[...] = _dot(h.astype(BF16), w2_ref[0])

    @pl.when(i >= nu_ref[0])
    def _():
        o_ref[...] = jnp.zeros(o_ref.shape, o_ref.dtype)


def expert_mlp(xs, tile_expert, n_used, w1, w3, w2, tm=EXPERT_TILE):
    r, d = xs.shape
    f = w1.shape[2]
    assert r % tm == 0
    nt = r // tm
    rows = lambda i, te, nu: (jnp.minimum(i, nu[0] - 1), 0)
    wsel = lambda i, te, nu: (te[jnp.minimum(i, nu[0] - 1)], 0, 0)
    grid_spec = pltpu.PrefetchScalarGridSpec(
        num_scalar_prefetch=2,
        grid=(nt,),
        in_specs=[pl.BlockSpec((tm, d), rows), pl.BlockSpec((1, d, f), wsel),
                  pl.BlockSpec((1, d, f), wsel), pl.BlockSpec((1, f, d), wsel)],
        out_specs=pl.BlockSpec((tm, d), lambda i, te, nu: (i, 0)),
    )
    return pl.pallas_call(
        _expert_kernel,
        grid_spec=grid_spec,
        out_shape=jax.ShapeDtypeStruct((r, d), F32),
        compiler_params=_params(("arbitrary",)),
        name="expert_mlp",
    )(tile_expert, n_used, xs, w1, w3, w2)


def _combine_kernel(dest_ref, gate_ref, h_ref, sh_ref, g_ref, b_ref, o_hbm, y_ref, buf, sem, *, tb, alpha):

    def start(r, carry):
        for k in range(MOE_TOPK):
            _row_copy(o_hbm, dest_ref[0, k, r], buf.at[k], r, sem).start()
        return carry

    lax.fori_loop(0, tb, start, 0, unroll=DMA_UNROLL)
    for k in range(MOE_TOPK):
        pltpu.make_async_copy(o_hbm.at[pl.ds(0, tb)], buf.at[k], sem).wait()
    gate = gate_ref[...]
    routed = gate[:, 0:1] * buf[0]
    for k in range(1, MOE_TOPK):
        routed = routed + gate[:, k:k + 1] * buf[k]
    z = alpha * h_ref[...] + (routed + sh_ref[...])
    y_ref[...] = _layer_norm(z, g_ref[...], b_ref[...])


def combine_ln(dest_tiles, gate_t, h, shared, o_sorted, g, b, alpha, tb=COMBINE_TILE):
    t, d = h.shape
    assert t % tb == 0
    row = lambda i: (i, 0)
    fixed = lambda i: (0, 0)
    return pl.pallas_call(
        functools.partial(_combine_kernel, tb=tb, alpha=alpha),
        grid=(t // tb,),
        in_specs=[pl.BlockSpec((1, SUBLANES, tb), lambda i: (i, 0, 0), memory_space=pltpu.SMEM),
                  pl.BlockSpec((tb, SUBLANES), row), pl.BlockSpec((tb, d), row), pl.BlockSpec((tb, d), row),
                  pl.BlockSpec((1, d), fixed), pl.BlockSpec((1, d), fixed),
                  pl.BlockSpec(memory_space=pl.ANY)],
        out_specs=pl.BlockSpec((tb, d), row),
        out_shape=jax.ShapeDtypeStruct((t, d), F32),
        scratch_shapes=[pltpu.VMEM((MOE_TOPK, tb, d), F32), pltpu.SemaphoreType.DMA],
        compiler_params=_params(("arbitrary",)),
        name="moe_combine",
    )(dest_tiles, gate_t, h, shared, g, b, o_sorted)


def moe_ffn_ln(h1, w_router, router_bias, w1, w3, w2, ws1, ws3, ws2, g, b, alpha):
    t, d = h1.shape
    idx, gate, rank, cnt = route(h1, w_router.T, router_bias)
    counts = cnt[:, 0]
    padded = (counts + EXPERT_TILE - 1) // EXPERT_TILE * EXPERT_TILE
    e_ids = jnp.arange(N_EXPERTS, dtype=I32)
    pad_end = jnp.sum(jnp.where(e_ids[None, :] <= e_ids[:, None], padded[None, :], 0), axis=1)
    pad_start = pad_end - padded
    n_tiles = -(-(t * MOE_TOPK) // EXPERT_TILE) + N_EXPERTS
    tile_start = jnp.arange(n_tiles, dtype=I32) * EXPERT_TILE
    tile_expert = jnp.minimum(jnp.sum((pad_end[None, :] <= tile_start[:, None]).astype(I32), axis=1),
                              N_EXPERTS - 1)
    n_used = (pad_end[-1] // EXPERT_TILE).astype(I32).reshape(1)
    dest = jnp.sum(jnp.where(idx[:, :, None] == e_ids, pad_start, 0), axis=-1) + rank

    def tiles(a, tb):
        return a.reshape(SUBLANES, t // tb, tb).transpose(1, 0, 2)

    xs = dispatch(h1, tiles(dest, SCATTER_TILE), pad_end.astype(I32), n_tiles * EXPERT_TILE)
    o_sorted = expert_mlp_f32(xs, tile_expert, n_used, w1, w3, w2)
    n_dense = t // ROUTE_TILE
    shared = expert_mlp_f32(h1, jnp.zeros((n_dense,), I32), jnp.full((1,), n_dense, I32),
                            ws1[None], ws3[None], ws2[None], tm=ROUTE_TILE)
    return combine_ln(tiles(dest, COMBINE_TILE), gate.T, h1, shared, o_sorted, g, b, alpha)


def _split_w_in(w_in, d_conv):
    d = w_in.shape[0]
    hd = N_HEADS * HEAD_DIM
    kvw = 2 * N_KV * HEAD_DIM
    o = 0
    parts = {}
    for name, width in (("b", d_conv), ("c", d_conv), ("h", d_conv), ("q", hd), ("cmp", kvw), ("sel", kvw),
                        ("win", kvw), ("nsa", 3 * N_HEADS), ("merge", 2 * d)):
        parts[name] = w_in[:, o:o + width]
        o += width
    nsa = parts["nsa"].reshape(d, N_KV, HPG * 3)
    parts["nsa"] = jnp.pad(nsa, ((0, 0), (0, 0), (0, LANES - HPG * 3))).reshape(d, N_KV * LANES)
    return {k: v.astype(BF16) for k, v in parts.items()}


def kernel(x_prompt, x_sample, cache_cmp_kv, cache_sel_kv, state_win_kv, state_conv, page_table,
           w_in, conv_w, w_phi_k, w_phi_v, w_conv_out, w_attn_out, w_o, ln1_g, ln1_b,
           w_router, router_bias, w_e_gate, w_e_up, w_e_down, w_s_gate, w_s_up, w_s_down,
           ln2_g, ln2_b):
    depth = w_in.shape[0]
    assert depth == 1
    alpha = (2.0 * depth) ** 0.25
    bp, tp, d = x_prompt.shape
    ns_, ts, _ = x_sample.shape
    assert ts == 1
    n_pages = page_table.shape[1]
    past_len = n_pages * PAGE_SIZE
    d_conv = conv_w.shape[2]
    kvw = 2 * N_KV * HEAD_DIM
    gw = N_KV * HEAD_DIM
    hd = N_HEADS * HEAD_DIM
    t_all = bp * tp + ns_

    w = _split_w_in(w_in[0], d_conv)
    cw = conv_w[0]
    wk_phi = w_phi_k[0].reshape(CMP_BLOCK * HEAD_DIM, HEAD_DIM).astype(BF16)
    wv_phi = w_phi_v[0].reshape(CMP_BLOCK * HEAD_DIM, HEAD_DIM).astype(BF16)
    wco = w_conv_out[0].astype(BF16)
    wao = w_attn_out[0].astype(BF16)
    wo = w_o[0].astype(BF16)
    g1, b1 = ln1_g[0].reshape(1, d), ln1_b[0].reshape(1, d)
    g2, b2 = ln2_g[0].reshape(1, d), ln2_b[0].reshape(1, d)

    xpb = x_prompt.astype(BF16)
    xp2 = xpb.reshape(bp * tp, d)
    ya_p, st_p = conv_proj(xpb, w["b"], w["c"], w["h"], cw, jnp.zeros((bp, SUBLANES, d_conv), F32))
    (q_p,) = matmul(xp2, w["q"], (BF16,))
    (cmp_p,) = kv_proj(xp2, w["cmp"], False)
    sel_p, sel_pb = kv_proj(xp2, w["sel"], True)
    win_p, win_pb = kv_proj(xp2, w["win"], True)
    (ng_p,) = matmul(xp2, w["nsa"], (F32,))
    nc_p = tp // CMP_BLOCK
    slabs = lambda a: a.reshape(-1, HEAD_DIM)
    kc_p, vc_p = cmp_kv(cmp_p, wk_phi, wv_phi, nb=nc_p)
    pad_c = lambda a: jnp.pad(a, ((0, 0), (0, 0), (0, LANES - a.shape[2]), (0, 0))).astype(BF16)
    blk_of_key = jnp.arange(tp, dtype=I32) // SEL_BLOCK
    expand = (jnp.arange(LANES, dtype=I32)[:, None] == 2 * blk_of_key[None, :]).astype(BF16)
    o_p = prompt_attention(q_p.reshape(bp, tp, hd), ng_p.reshape(bp, tp, N_KV * LANES), pad_c(kc_p), pad_c(vc_p),
                           sel_pb.reshape(bp, tp, kvw), win_pb.reshape(bp, tp, kvw), expand)
    mix_p = merge_mixers(xp2, ya_p.reshape(bp * tp, d_conv), o_p.reshape(bp * tp, hd), w["merge"], wco, wao)
    h1 = out_proj_ln(mix_p, wo, x_prompt.reshape(bp * tp, d), g1, b1, alpha, t_all, 0,
                     prev=jnp.zeros((t_all, d), F32))

    xs2 = x_sample.reshape(ns_, d)
    xsb = xs2.astype(BF16)
    w_conv3 = jnp.concatenate([w["b"], w["c"], w["h"]], axis=1)
    (pconv_s,) = matmul(xsb, w_conv3, (F32,))
    ya_s, u_s = step_conv(pconv_s, state_conv[0, :, 0], state_conv[0, :, 1], cw)
    (q_s,) = matmul(xsb, w["q"], (BF16,))
    (cmp_s,) = matmul(xsb, w["cmp"], (F32,))
    (sel_s,) = matmul(xsb, w["sel"], (F32,))
    (win_s,) = matmul(xsb, w["win"], (F32,))
    (ng_s,) = matmul(xsb, w["nsa"], (F32,))
    n_phys = cache_cmp_kv.shape[1]
    bpp = PAGE_SIZE // CMP_BLOCK
    pages_per_tile = 32
    assert n_phys % pages_per_tile == 0
    kc_all, vc_all = cmp_kv(slabs(cache_cmp_kv), wk_phi, wv_phi, nb=pages_per_tile * bpp)

    def per_sequence(a):
        a = a.reshape(n_phys // pages_per_tile, N_KV, pages_per_tile, bpp, HEAD_DIM)
        a = a.transpose(0, 2, 3, 1, 4).reshape(n_phys, bpp, gw)
        a = a[page_table].reshape(ns_, n_pages * bpp, gw)
        return jnp.pad(a, ((0, 0), (0, LANES - n_pages * bpp), (0, 0))).astype(BF16)

    head_group = jnp.arange(LANES, dtype=I32) // HPG
    qh = jnp.pad(q_s.reshape(ns_, N_HEADS, HEAD_DIM), ((0, 0), (0, LANES - N_HEADS), (0, 0)))
    qblk = jnp.where(jnp.arange(N_KV, dtype=I32)[None, :, None, None] == head_group[None, None, None, :],
                     qh.transpose(0, 2, 1)[:, None], 0).reshape(ns_, gw, LANES).astype(BF16)
    nk = n_pages * PAGE_SIZE
    key_row = jnp.arange(nk + 2 * SUBLANES, dtype=I32)
    expand_t = ((2 * (key_row // SEL_BLOCK))[:, None] == jnp.arange(LANES, dtype=I32)[None, :]) & (key_row <= nk)[:, None]
    o_s, win_next = step_attention(qblk, ng_s.reshape(ns_, 1, N_KV * LANES), per_sequence(kc_all), per_sequence(vc_all),
                         slabs(cache_sel_kv), page_table, sel_s.reshape(ns_, 1, kvw), slabs(state_win_kv),
                         win_s.reshape(ns_, 1, kvw), expand_t.astype(BF16), past_len)
    mix_s = merge_mixers(xsb, ya_s, o_s.reshape(ns_, hd), w["merge"], wco, wao)
    h1 = out_proj_ln(mix_s, wo, xs2, g1, b1, alpha, t_all, bp * tp, prev=h1)

    y = moe_ffn_ln(h1, w_router[0], router_bias[0], w_e_gate[0], w_e_up[0], w_e_down[0],
                   w_s_gate[0], w_s_up[0], w_s_down[0], g2, b2, alpha)

    kv6 = lambda a, n, t: a.reshape(1, n, t, 2, N_KV, HEAD_DIM)
    win_keep = min(WINDOW, tp)
    new_win_s = win_next.reshape(state_win_kv.shape)
    new_conv_s = jnp.stack([state_conv[0, :, 1], u_s], axis=1)[None]
    return (y[:bp * tp].reshape(bp, tp, d), y[bp * tp:].reshape(ns_, ts, d),
            kv6(cmp_p, bp, tp), kv6(sel_p, bp, tp), kv6(win_p, bp, tp)[:, :, tp - win_keep:],
            st_p[None, :, SUBLANES - (CONV_WIDTH - 1):],
            kv6(cmp_s, ns_, 1), kv6(sel_s, ns_, 1), new_win_s, new_conv_s)
```

```python
import functools

import jax
import jax.numpy as jnp
from jax import lax
from jax.experimental import pallas as pl
from jax.experimental.pallas import tpu as pltpu

F32 = jnp.float32
BF16 = jnp.bfloat16
I32 = jnp.int32

N_HEADS = 16
HEAD_DIM = 128
N_KV = 4
HPG = N_HEADS // N_KV
CMP_BLOCK = 32
SEL_BLOCK = 64
SEL_TOPK = 8
WINDOW = 512
PAGE_SIZE = 128
CONV_WIDTH = 3
N_EXPERTS = 64
MOE_TOPK = 6
N_GROUPS = 8
TOPK_GROUPS = 4
ROUTED_SCALE = 2.5
LN_EPS = 1e-5
NEG_INF = -1e30
FORCE_SCORE = 1e4
ATTN_SCALE = HEAD_DIM ** -0.5
LOG2_E = 1.4426950408889634

LANES = 128
SUBLANES = 8
VMEM_LIMIT = 56 * 1024 * 1024

EXPERT_TILE = 256
ROUTE_TILE = 384
COMBINE_TILE = 64
SCATTER_TILE = 128
DMA_UNROLL = 8
DMA_PRIORITIES = 2


def _dot(a, b):
    return jnp.dot(a, b, preferred_element_type=F32)


def _dot_nt(a, b):
    return lax.dot_general(a, b, (((1,), (1,)), ((), ())), preferred_element_type=F32)


def _dot_tn(a, b):
    return lax.dot_general(a, b, (((0,), (0,)), ((), ())), preferred_element_type=F32)


def _params(sem):
    return pltpu.CompilerParams(dimension_semantics=sem, vmem_limit_bytes=VMEM_LIMIT)


def _mm_kernel(x_ref, w_ref, *o_refs):
    r = _dot(x_ref[...], w_ref[...])
    for o_ref in o_refs:
        o_ref[...] = r.astype(o_ref.dtype)


def matmul(x, w, out_dtypes, tm=1024, tn=1024):
    m, k = x.shape
    n = w.shape[1]
    tm = min(tm, m)
    tn = min(tn, n)
    assert m % tm == 0 and n % tn == 0
    outs = pl.pallas_call(
        _mm_kernel,
        grid=(n // tn, m // tm),
        in_specs=[pl.BlockSpec((tm, k), lambda j, i: (i, 0)),
                  pl.BlockSpec((k, tn), lambda j, i: (0, j))],
        out_specs=[pl.BlockSpec((tm, tn), lambda j, i: (i, j)) for _ in out_dtypes],
        out_shape=[jax.ShapeDtypeStruct((m, n), dt) for dt in out_dtypes],
        compiler_params=_params(("parallel", "parallel")),
        name="proj_mm",
    )(x, w)
    return outs


def _kv_proj_kernel(x_ref, w_ref, slab_ref, *b_refs):
    r = _dot(x_ref[...], w_ref[...])
    tm, n = r.shape
    nslab = n // HEAD_DIM
    for c in range(nslab):
        slab_ref[pl.ds(c, tm, stride=nslab), :] = r[:, c * HEAD_DIM:(c + 1) * HEAD_DIM]
    for b_ref in b_refs:
        b_ref[...] = r.astype(b_ref.dtype)


def kv_proj(x, w, with_bf16, tm=1024):
    m, k = x.shape
    n = w.shape[1]
    nslab = n // HEAD_DIM
    tm = min(tm, m)
    assert m % tm == 0
    out_specs = [pl.BlockSpec((tm * nslab, HEAD_DIM), lambda i: (i, 0))]
    out_shape = [jax.ShapeDtypeStruct((m * nslab, HEAD_DIM), F32)]
    if with_bf16:
        out_specs.append(pl.BlockSpec((tm, n), lambda i: (i, 0)))
        out_shape.append(jax.ShapeDtypeStruct((m, n), BF16))
    return pl.pallas_call(
        _kv_proj_kernel,
        grid=(m // tm,),
        in_specs=[pl.BlockSpec((tm, k), lambda i: (i, 0)), pl.BlockSpec((k, n), lambda i: (0, 0))],
        out_specs=out_specs,
        out_shape=out_shape,
        compiler_params=_params(("parallel",)),
        name="kv_proj",
    )(x, w)


def _conv_proj_kernel(x_ref, wb_ref, wc_ref, wh_ref, cw_ref, pre_ref, ya_ref, st_ref, s_ref, *, tm):
    i = pl.program_id(2)

    @pl.when(i == 0)
    def _():
        s_ref[0:SUBLANES, :] = pre_ref[0]

    x = x_ref[0]
    pb = _dot(x, wb_ref[...])
    u = _dot(x, wc_ref[...]) * _dot(x, wh_ref[...])
    s_ref[SUBLANES:SUBLANES + tm, :] = u
    u1 = s_ref[SUBLANES - 1:SUBLANES - 1 + tm, :]
    u2 = s_ref[SUBLANES - 2:SUBLANES - 2 + tm, :]
    cw = cw_ref[...]
    y = cw[0:1] * u2 + cw[1:2] * u1 + cw[2:3] * u
    ya_ref[0] = (pb * y).astype(ya_ref.dtype)
    last = s_ref[tm:tm + SUBLANES, :]
    s_ref[0:SUBLANES, :] = last
    st_ref[0] = last


def conv_proj(xb, wb, wc, wh, conv_w, prefix8, tm=512, tn=512):
    b, t, d = xb.shape
    dc = wb.shape[1]
    tm = min(tm, t)
    assert t % tm == 0 and dc % tn == 0 and tm >= SUBLANES
    return pl.pallas_call(
        functools.partial(_conv_proj_kernel, tm=tm),
        grid=(dc // tn, b, t // tm),
        in_specs=[pl.BlockSpec((1, tm, d), lambda j, bb, i: (bb, i, 0)),
                  pl.BlockSpec((d, tn), lambda j, bb, i: (0, j)),
                  pl.BlockSpec((d, tn), lambda j, bb, i: (0, j)),
                  pl.BlockSpec((d, tn), lambda j, bb, i: (0, j)),
                  pl.BlockSpec((CONV_WIDTH, tn), lambda j, bb, i: (0, j)),
                  pl.BlockSpec((1, SUBLANES, tn), lambda j, bb, i: (bb, 0, j))],
        out_specs=[pl.BlockSpec((1, tm, tn), lambda j, bb, i: (bb, i, j)),
                   pl.BlockSpec((1, SUBLANES, tn), lambda j, bb, i: (bb, 0, j))],
        out_shape=[jax.ShapeDtypeStruct((b, t, dc), BF16),
                   jax.ShapeDtypeStruct((b, SUBLANES, dc), F32)],
        scratch_shapes=[pltpu.VMEM((tm + SUBLANES, tn), F32)],
        compiler_params=_params(("parallel", "parallel", "arbitrary")),
        name="conv_proj",
    )(xb, wb, wc, wh, conv_w, prefix8)


def _step_conv_kernel(p_ref, s0_ref, s1_ref, cw_ref, ya_ref, u_ref, *, dc):
    pb = p_ref[:, 0:dc]
    u = p_ref[:, dc:2 * dc] * p_ref[:, 2 * dc:3 * dc]
    cw = cw_ref[...]
    y = cw[0:1] * s0_ref[...] + cw[1:2] * s1_ref[...] + cw[2:3] * u
    ya_ref[...] = (pb * y).astype(ya_ref.dtype)
    u_ref[...] = u


def step_conv(p, s0, s1, conv_w):
    n, dc3 = p.shape
    dc = dc3 // 3
    return pl.pallas_call(
        functools.partial(_step_conv_kernel, dc=dc),
        out_shape=[jax.ShapeDtypeStruct((n, dc), BF16), jax.ShapeDtypeStruct((n, dc), F32)],
        compiler_params=pltpu.CompilerParams(vmem_limit_bytes=VMEM_LIMIT),
        name="step_conv",
    )(p, s0, s1, conv_w)


def _cmp_kv_kernel(x_ref, wk_ref, wv_ref, ok_ref, ov_ref, *, nb):
    nslab = 2 * N_KV

    def rows(l, c):
        return x_ref[pl.ds(l * nslab + c, nb, stride=CMP_BLOCK * nslab), :].astype(BF16)

    for kv, (w_ref, o_ref) in enumerate(((wk_ref, ok_ref), (wv_ref, ov_ref))):
        per_group = [jnp.concatenate([rows(l, kv * N_KV + g) for l in range(CMP_BLOCK)], axis=1)
                     for g in range(N_KV)]
        lhs = jnp.concatenate(per_group, axis=0)
        r = _dot(lhs, w_ref[...])
        for g in range(N_KV):
            o_ref[0, g] = r[g * nb:(g + 1) * nb]


def cmp_kv(raw, wk, wv, nb):
    n_rows, width = raw.shape
    rows_per_step = nb * CMP_BLOCK * 2 * N_KV
    assert n_rows % rows_per_step == 0 and width == HEAD_DIM
    nt = n_rows // rows_per_step
    return pl.pallas_call(
        functools.partial(_cmp_kv_kernel, nb=nb),
        grid=(nt,),
        in_specs=[pl.BlockSpec((rows_per_step, width), lambda i: (i, 0)),
                  pl.BlockSpec(wk.shape, lambda i: (0, 0)),
                  pl.BlockSpec(wv.shape, lambda i: (0, 0))],
        out_specs=[pl.BlockSpec((1, N_KV, nb, HEAD_DIM), lambda i: (i, 0, 0, 0))] * 2,
        out_shape=[jax.ShapeDtypeStruct((nt, N_KV, nb, HEAD_DIM), F32)] * 2,
        compiler_params=_params(("parallel",)),
        name="cmp_kv",
    )(raw, wk, wv)


def _top_k_mask(val, iota, axis, k, size):
    sel = jnp.zeros(val.shape, F32)
    iota = iota.astype(F32)
    for _ in range(k):
        m = jnp.max(val, axis=axis, keepdims=True)
        first = jnp.min(jnp.where(val == m, iota, float(size)), axis=axis, keepdims=True)
        hit = iota == first
        sel = jnp.where(hit, 1.0, sel)
        val = jnp.where(hit, -jnp.inf, val)
    return sel


def _prompt_attn_kernel(q_ref, ng_ref, kc_ref, vc_ref, ks_ref, vs_ref, kw_ref, vw_ref, e_ref, o_ref,
                        m_ref, l_ref, acc_ref, oacc_ref, *, tq, tks, tkw, nc, ns):
    qt = pl.program_id(2)
    t0 = qt * tq
    qpos = t0 + lax.broadcasted_iota(I32, (tq, 1), 0)
    gates = jax.nn.sigmoid(ng_ref[0])

    qcol = t0 + lax.broadcasted_iota(I32, (1, tq), 1)
    rowb = lax.broadcasted_iota(I32, (LANES, tq), 0)
    vis = ((rowb + 1) * CMP_BLOCK - 1 <= qcol) & (rowb < nc)
    visf = vis.astype(F32)
    kc = kc_ref[0, 0]
    vc = vc_ref[0, 0]
    imp = jnp.zeros((LANES, tq), F32)
    for h in range(HPG):
        qh = q_ref[0, :, h * HEAD_DIM:(h + 1) * HEAD_DIM]
        s = jnp.where(vis, _dot_nt(kc, qh) * ATTN_SCALE, NEG_INF)
        e = jnp.exp(s - jnp.max(s, axis=0, keepdims=True))
        p = e / jnp.sum(e, axis=0, keepdims=True) * visf
        imp = imp + p
        oacc_ref[h] = gates[:, h * 3:h * 3 + 1] * _dot_tn(p.astype(BF16), vc)

    pair = imp + pltpu.roll(imp, LANES - 1, axis=0)
    blk = rowb >> 1
    cand = ((rowb & 1) == 0) & (blk < ns)
    cur = qcol >> (SEL_BLOCK.bit_length() - 1)
    forced = (blk == 0) | (blk == cur) | (blk == cur - 1)
    val = jnp.where(blk * SEL_BLOCK <= qcol, jnp.where(forced, FORCE_SCORE, pair), -1.0)
    val = jnp.where(cand, val, -jnp.inf)
    selb = _top_k_mask(val, rowb, 0, SEL_TOPK, LANES).astype(BF16)

    def flash(k_ref, v_ref, tk, kt_lo, kt_hi, mask_fn, branch):
        m_ref[...] = jnp.full(m_ref.shape, NEG_INF, F32)
        l_ref[...] = jnp.zeros(l_ref.shape, F32)
        acc_ref[...] = jnp.zeros(acc_ref.shape, F32)

        def body(kt, carry):
            k0 = pl.multiple_of(kt * tk, tk)
            k = k_ref[0, pl.ds(k0, tk), :]
            v = v_ref[0, pl.ds(k0, tk), :]
            mask = mask_fn(k0, tk)
            for h in range(HPG):
                qh = q_ref[0, :, h * HEAD_DIM:(h + 1) * HEAD_DIM]
                s = jnp.where(mask, _dot_nt(qh, k) * (ATTN_SCALE * LOG2_E), NEG_INF)
                m_prev = m_ref[h]
                m_next = jnp.maximum(m_prev, jnp.max(s, axis=-1, keepdims=True))
                alpha = jnp.exp2(m_prev - m_next)
                p = jnp.exp2(s - jnp.tile(m_next, (1, tk // LANES)))
                psum = p[:, 0:LANES]
                for c in range(1, tk // LANES):
                    psum = psum + p[:, c * LANES:(c + 1) * LANES]
                l_ref[h] = alpha * l_ref[h] + psum
                acc_ref[h] = alpha * acc_ref[h] + _dot(p.astype(BF16), v)
                m_ref[h] = m_next
            return carry

        lax.fori_loop(kt_lo, kt_hi, body, 0)
        for h in range(HPG):
            c = h * 3 + branch
            l_tot = jnp.sum(l_ref[h], axis=-1, keepdims=True)
            oacc_ref[h] = oacc_ref[h] + gates[:, c:c + 1] * (acc_ref[h] / l_tot)

    def sel_mask(k0, tk):
        kpos = k0 + lax.broadcasted_iota(I32, (1, tk), 1)
        chosen = _dot_tn(selb, e_ref[:, pl.ds(k0, tk)]) > 0.5
        return chosen & (kpos <= qpos)

    def win_mask(k0, tk):
        kpos = k0 + lax.broadcasted_iota(I32, (1, tk), 1)
        return (kpos <= qpos) & (kpos >= qpos - WINDOW)

    last = t0 + tq - 1
    flash(ks_ref, vs_ref, tks, 0, last // tks + 1, sel_mask, 1)
    flash(kw_ref, vw_ref, tkw, jnp.maximum(t0 - WINDOW, 0) // tkw, last // tkw + 1, win_mask, 2)
    for h in range(HPG):
        o_ref[0, :, h * HEAD_DIM:(h + 1) * HEAD_DIM] = oacc_ref[h].astype(o_ref.dtype)


def prompt_attention(q, ng, kc, vc, sel_b, win_b, expand, tq=512, tks=256, tkw=256):
    b, t, _ = q.shape
    tks = min(tks, t)
    assert t % tq == 0 and t % tks == 0 and t % tkw == 0
    nc = t // CMP_BLOCK
    ns = max(-(-t // SEL_BLOCK), SEL_TOPK)
    assert nc <= LANES and 2 * ns <= LANES and ns * SEL_BLOCK == t
    gw = HPG * HEAD_DIM
    kern = functools.partial(_prompt_attn_kernel, tq=tq, tks=tks, tkw=tkw, nc=nc, ns=ns)
    kv_spec = lambda off: pl.BlockSpec((1, t, HEAD_DIM), lambda bb, g, i: (bb, 0, off + g))
    return pl.pallas_call(
        kern,
        grid=(b, N_KV, t // tq),
        in_specs=[pl.BlockSpec((1, tq, gw), lambda bb, g, i: (bb, i, g)),
                  pl.BlockSpec((1, tq, LANES), lambda bb, g, i: (bb, i, g)),
                  pl.BlockSpec((1, 1, LANES, HEAD_DIM), lambda bb, g, i: (bb, g, 0, 0)),
                  pl.BlockSpec((1, 1, LANES, HEAD_DIM), lambda bb, g, i: (bb, g, 0, 0)),
                  kv_spec(0), kv_spec(N_KV), kv_spec(0), kv_spec(N_KV),
                  pl.BlockSpec((LANES, t), lambda bb, g, i: (0, 0))],
        out_specs=pl.BlockSpec((1, tq, gw), lambda bb, g, i: (bb, i, g)),
        out_shape=jax.ShapeDtypeStruct(q.shape, BF16),
        scratch_shapes=[pltpu.VMEM((HPG, tq, LANES), F32), pltpu.VMEM((HPG, tq, LANES), F32),
                        pltpu.VMEM((HPG, tq, HEAD_DIM), F32), pltpu.VMEM((HPG, tq, HEAD_DIM), F32)],
        compiler_params=_params(("parallel", "parallel", "arbitrary")),
        name="prompt_attn",
    )(q, ng, kc, vc, sel_b, sel_b, win_b, win_b, expand)


def _step_attn_kernel(*refs, n_pages, past_len, nc, ns):
    (_, qb_ref, ng_ref, kc_ref, vc_ref) = refs[:5]
    pages = refs[5:5 + n_pages]
    (snew_ref, win_ref, wnew_ref, wslab_ref, et_ref, o_ref, wout_ref, kbuf, vbuf, kwbuf, vwbuf) = refs[5 + n_pages:]
    gw = N_KV * HEAD_DIM
    qb = qb_ref[0]
    lane = lax.broadcasted_iota(I32, (LANES, LANES), 1)
    rowi = lax.broadcasted_iota(I32, (LANES, LANES), 0)

    def attend(k, v, mask):
        s = jnp.where(mask, _dot(k, qb) * ATTN_SCALE, NEG_INF)
        e = jnp.exp(s - jnp.max(s, axis=0, keepdims=True))
        p = e / jnp.sum(e, axis=0, keepdims=True) * mask.astype(F32)
        return p, _dot_tn(p.astype(BF16), v)

    vis = ((rowi + 1) * CMP_BLOCK - 1 <= past_len) & (rowi < nc)
    pc, oc = attend(kc_ref[0], vc_ref[0], vis)

    a = pc + pltpu.roll(pc, LANES - 1, axis=1)
    a = a + pltpu.roll(a, LANES - 2, axis=1)
    pair = a + pltpu.roll(a, LANES - 1, axis=0)
    blk = rowi >> 1
    cand = ((rowi & 1) == 0) & (blk < ns)
    cur = past_len // SEL_BLOCK
    forced = (blk == 0) | (blk == cur) | (blk == cur - 1)
    val = jnp.where(blk * SEL_BLOCK <= past_len, jnp.where(forced, FORCE_SCORE, pair), -1.0)
    val = jnp.where(cand, val, -jnp.inf)
    sel = _top_k_mask(val, rowi, 0, SEL_TOPK, LANES)
    sel = jnp.where(((lane & (HPG - 1)) == 0) & (lane < N_HEADS), sel, 0.0)
    selh = sel
    for r in range(1, HPG):
        selh = selh + pltpu.roll(sel, r, axis=1)

    nk = n_pages * PAGE_SIZE
    nslab = 2 * N_KV
    tail = lax.broadcasted_iota(I32, (2 * SUBLANES, gw), 0) == 0
    for p in range(n_pages):
        for g in range(N_KV):
            cols = slice(g * HEAD_DIM, (g + 1) * HEAD_DIM)
            kbuf[p * PAGE_SIZE:(p + 1) * PAGE_SIZE, cols] = pages[p][pl.ds(g, PAGE_SIZE, stride=nslab), :].astype(BF16)
            vbuf[p * PAGE_SIZE:(p + 1) * PAGE_SIZE, cols] = pages[p][pl.ds(N_KV + g, PAGE_SIZE, stride=nslab), :].astype(BF16)
    kbuf[nk:nk + 2 * SUBLANES, :] = jnp.where(tail, snew_ref[0, :, 0:gw], 0.0).astype(BF16)
    vbuf[nk:nk + 2 * SUBLANES, :] = jnp.where(tail, snew_ref[0, :, gw:2 * gw], 0.0).astype(BF16)
    chosen = _dot(et_ref[...], selh.astype(BF16)) > 0.5
    _, osel = attend(kbuf[...], vbuf[...], chosen)

    nw = win_ref.shape[0] // nslab
    for g in range(N_KV):
        cols = slice(g * HEAD_DIM, (g + 1) * HEAD_DIM)
        kwbuf[0:nw, cols] = win_ref[pl.ds(g, nw, stride=nslab), :].astype(BF16)
        vwbuf[0:nw, cols] = win_ref[pl.ds(N_KV + g, nw, stride=nslab), :].astype(BF16)
    kwbuf[nw:nw + 2 * SUBLANES, :] = jnp.where(tail, wnew_ref[0, :, 0:gw], 0.0).astype(BF16)
    vwbuf[nw:nw + 2 * SUBLANES, :] = jnp.where(tail, wnew_ref[0, :, gw:2 * gw], 0.0).astype(BF16)
    wrow = lax.broadcasted_iota(I32, (nw + 2 * SUBLANES, LANES), 0)
    _, owin = attend(kwbuf[...], vwbuf[...], wrow <= nw)
    wout_ref[0:(nw - 1) * nslab, :] = win_ref[nslab:nw * nslab, :]
    wout_ref[(nw - 1) * nslab:nw * nslab, :] = wslab_ref[...]

    gates = jax.nn.sigmoid(ng_ref[0])
    for h in range(N_HEADS):
        g, hh = divmod(h, HPG)
        c = g * LANES + hh * 3
        cols = slice(g * HEAD_DIM, (g + 1) * HEAD_DIM)
        o_h = (gates[:, c:c + 1] * oc[h:h + 1, cols] + gates[:, c + 1:c + 2] * osel[h:h + 1, cols]
               + gates[:, c + 2:c + 3] * owin[h:h + 1, cols])
        o_ref[0, :, h * HEAD_DIM:(h + 1) * HEAD_DIM] = o_h.astype(o_ref.dtype)


def step_attention(qblk, ng, kc, vc, sel_pool, page_table, sel_new, win_state, win_new, expand_t, past_len):
    n, n_pages = page_table.shape
    gw = N_KV * HEAD_DIM
    nslab = 2 * N_KV
    nk = n_pages * PAGE_SIZE
    nw = win_state.shape[0] // (n * nslab)
    nc = (past_len + 1) // CMP_BLOCK
    ns = max(-(-(past_len + 1) // SEL_BLOCK), SEL_TOPK)
    assert nc <= LANES and 2 * ns <= LANES and nw == WINDOW and past_len >= WINDOW
    kern = functools.partial(_step_attn_kernel, n_pages=n_pages, past_len=past_len, nc=nc, ns=ns)
    page_specs = [pl.BlockSpec((PAGE_SIZE * nslab, HEAD_DIM), functools.partial(lambda i, pt, p: (pt[i, p], 0), p=p))
                  for p in range(n_pages)]
    grid_spec = pltpu.PrefetchScalarGridSpec(
        num_scalar_prefetch=1,
        grid=(n,),
        in_specs=[pl.BlockSpec((1, gw, LANES), lambda i, pt: (i, 0, 0)),
                  pl.BlockSpec((1, 1, N_KV * LANES), lambda i, pt: (i, 0, 0)),
                  pl.BlockSpec((1, LANES, gw), lambda i, pt: (i, 0, 0)),
                  pl.BlockSpec((1, LANES, gw), lambda i, pt: (i, 0, 0))]
        + page_specs
        + [pl.BlockSpec((1, 1, 2 * gw), lambda i, pt: (i, 0, 0)),
           pl.BlockSpec((nw * nslab, HEAD_DIM), lambda i, pt: (i, 0)),
           pl.BlockSpec((1, 1, 2 * gw), lambda i, pt: (i, 0, 0)),
           pl.BlockSpec((nslab, HEAD_DIM), lambda i, pt: (i, 0)),
           pl.BlockSpec((nk + 2 * SUBLANES, LANES), lambda i, pt: (0, 0))],
        out_specs=[pl.BlockSpec((1, 1, N_HEADS * HEAD_DIM), lambda i, pt: (i, 0, 0)),
                   pl.BlockSpec((nw * nslab, HEAD_DIM), lambda i, pt: (i, 0))],
        scratch_shapes=[pltpu.VMEM((nk + 2 * SUBLANES, gw), BF16), pltpu.VMEM((nk + 2 * SUBLANES, gw), BF16),
                        pltpu.VMEM((nw + 2 * SUBLANES, gw), BF16), pltpu.VMEM((nw + 2 * SUBLANES, gw), BF16)],
    )
    return pl.pallas_call(
        kern,
        grid_spec=grid_spec,
        out_shape=[jax.ShapeDtypeStruct((n, 1, N_HEADS * HEAD_DIM), BF16),
                   jax.ShapeDtypeStruct(win_state.shape, F32)],
        compiler_params=_params(("arbitrary",)),
        name="step_attn",
    )(page_table, qblk, ng, kc, vc, *([sel_pool] * n_pages), sel_new, win_state, win_new,
      win_new.reshape(n * nslab, HEAD_DIM), expand_t)


def _merge_kernel(x_ref, ya_ref, ob_ref, wga_ref, wgb_ref, wco_ref, wao_ref, o_ref):
    x = x_ref[...]
    ga = jax.nn.sigmoid(_dot(x, wga_ref[...]))
    gb = jax.nn.sigmoid(_dot(x, wgb_ref[...]))
    y_a = _dot(ya_ref[...], wco_ref[...])
    y_b = _dot(ob_ref[...], wao_ref[...])
    o_ref[...] = (ga * y_a + gb * y_b).astype(o_ref.dtype)


def merge_mixers(xb, ya_in, ob, w_merge, w_conv_out, w_attn_out, tm=512, tn=512):
    m, d = xb.shape
    tm = min(tm, m)
    assert m % tm == 0 and d % tn == 0
    nj = d // tn
    row = lambda j, i: (i, 0)
    col = lambda j, i: (0, j)
    return pl.pallas_call(
        _merge_kernel,
        grid=(nj, m // tm),
        in_specs=[pl.BlockSpec((tm, d), row), pl.BlockSpec((tm, ya_in.shape[1]), row),
                  pl.BlockSpec((tm, ob.shape[1]), row),
                  pl.BlockSpec((d, tn), col), pl.BlockSpec((d, tn), lambda j, i: (0, j + nj)),
                  pl.BlockSpec((w_conv_out.shape[0], tn), col), pl.BlockSpec((w_attn_out.shape[0], tn), col)],
        out_specs=pl.BlockSpec((tm, tn), lambda j, i: (i, j)),
        out_shape=jax.ShapeDtypeStruct((m, d), BF16),
        compiler_params=_params(("parallel", "parallel")),
        name="merge_mixers",
    )(xb, ya_in, ob, w_merge, w_merge, w_conv_out, w_attn_out)


def _layer_norm(z, g, b):
    mu = jnp.mean(z, axis=-1, keepdims=True)
    zc = z - mu
    var = jnp.mean(zc * zc, axis=-1, keepdims=True)
    return zc * lax.rsqrt(var + LN_EPS) * g + b


def _out_proj_ln_kernel(m_ref, w_ref, x_ref, g_ref, b_ref, *rest, alpha):
    o_ref = rest[-1]
    z = alpha * x_ref[...] + _dot(m_ref[...], w_ref[...])
    o_ref[...] = _layer_norm(z, g_ref[...], b_ref[...])


def out_proj_ln(mixin, w_o, x, g, b, alpha, total_rows, row_offset, prev=None, tm=512):
    m, d = x.shape
    tm = min(tm, m)
    assert m % tm == 0 and row_offset % tm == 0
    off = row_offset // tm
    row = lambda i: (i, 0)
    fixed = lambda i: (0, 0)
    in_specs = [pl.BlockSpec((tm, d), row), pl.BlockSpec((d, d), fixed), pl.BlockSpec((tm, d), row),
                pl.BlockSpec((1, d), fixed), pl.BlockSpec((1, d), fixed)]
    args = [mixin, w_o, x, g, b]
    aliases = {}
    if prev is not None:
        in_specs.append(pl.BlockSpec(memory_space=pl.ANY))
        args.append(prev)
        aliases = {5: 0}
    return pl.pallas_call(
        functools.partial(_out_proj_ln_kernel, alpha=alpha),
        grid=(m // tm,),
        in_specs=in_specs,
        out_specs=pl.BlockSpec((tm, d), lambda i: (i + off, 0)),
        out_shape=jax.ShapeDtypeStruct((total_rows, d), F32),
        input_output_aliases=aliases,
        compiler_params=_params(("parallel",)),
        name="out_proj_ln",
    )(*args)


def _route_kernel(h_ref, wr_ref, rb_ref, u_ref, idx_ref, gate_ref, rank_ref, cnt_ref, run_ref, *, tm):
    i = pl.program_id(0)

    @pl.when(i == 0)
    def _():
        run_ref[...] = jnp.zeros(run_ref.shape, F32)

    logits = lax.dot_general(wr_ref[...], h_ref[...], (((1,), (1,)), ((), ())),
                             precision=lax.Precision.HIGHEST, preferred_element_type=F32)
    s = jax.nn.sigmoid(logits)
    bsc = s + rb_ref[...]
    gsz = N_EXPERTS // N_GROUPS
    x3 = bsc.reshape(N_GROUPS, gsz, tm)
    sub = lax.broadcasted_iota(I32, (N_GROUPS, gsz, tm), 1).astype(F32)
    m1 = jnp.max(x3, axis=1, keepdims=True)
    i1 = jnp.min(jnp.where(x3 == m1, sub, float(gsz)), axis=1, keepdims=True)
    m2 = jnp.max(jnp.where(sub == i1, -jnp.inf, x3), axis=1, keepdims=True)
    gsc = (m1 + m2).reshape(N_GROUPS, tm)
    giota = lax.broadcasted_iota(I32, (N_GROUPS, tm), 0)
    gsel = _top_k_mask(gsc, giota, 0, TOPK_GROUPS, N_GROUPS)
    emask = jnp.broadcast_to(gsel.reshape(N_GROUPS, 1, tm), (N_GROUPS, gsz, tm)).reshape(N_EXPERTS, tm) > 0.5
    val = jnp.where(emask, bsc, NEG_INF)
    eiota = lax.broadcasted_iota(I32, (N_EXPERTS, tm), 0).astype(F32)
    hits, idxs, sels = [], [], []
    for _ in range(MOE_TOPK):
        m = jnp.max(val, axis=0, keepdims=True)
        first = jnp.min(jnp.where(val == m, eiota, float(N_EXPERTS)), axis=0, keepdims=True)
        hit = eiota == first
        hits.append(hit)
        idxs.append(first.astype(I32))
        sels.append(jnp.sum(jnp.where(hit, s, 0.0), axis=0, keepdims=True))
        val = jnp.where(hit, -jnp.inf, val)
    den = sels[0]
    for k in range(1, MOE_TOPK):
        den = den + sels[k]
    onehot = hits[0].astype(F32)
    for k in range(1, MOE_TOPK):
        onehot = onehot + hits[k].astype(F32)
    before = _dot(onehot.astype(BF16), u_ref[...]) + run_ref[:, 0:1]
    pad = SUBLANES - MOE_TOPK
    zi = jnp.zeros((pad, tm), I32)
    zf = jnp.zeros((pad, tm), F32)
    ranks = [jnp.sum(jnp.where(hits[k], before, 0.0), axis=0, keepdims=True).astype(I32) for k in range(MOE_TOPK)]
    idx_ref[...] = jnp.concatenate(idxs + [zi], axis=0)
    gate_ref[...] = jnp.concatenate([sels[k] / den * ROUTED_SCALE for k in range(MOE_TOPK)] + [zf], axis=0)
    rank_ref[...] = jnp.concatenate(ranks + [zi], axis=0)
    run_ref[...] = run_ref[...] + jnp.sum(onehot, axis=1, keepdims=True)
    cnt_ref[...] = run_ref[...].astype(I32)


def route(h, w_router_t, router_bias, tm=ROUTE_TILE):
    t, d = h.shape
    assert t % tm == 0
    upper = (lax.broadcasted_iota(I32, (tm, tm), 0) < lax.broadcasted_iota(I32, (tm, tm), 1)).astype(BF16)
    tok = lambda i: (0, i)
    fixed = lambda i: (0, 0)
    return pl.pallas_call(
        functools.partial(_route_kernel, tm=tm),
        grid=(t // tm,),
        in_specs=[pl.BlockSpec((tm, d), lambda i: (i, 0)), pl.BlockSpec((N_EXPERTS, d), fixed),
                  pl.BlockSpec((N_EXPERTS, 1), fixed), pl.BlockSpec((tm, tm), fixed)],
        out_specs=[pl.BlockSpec((SUBLANES, tm), tok), pl.BlockSpec((SUBLANES, tm), tok),
                   pl.BlockSpec((SUBLANES, tm), tok), pl.BlockSpec((N_EXPERTS, LANES), fixed)],
        out_shape=[jax.ShapeDtypeStruct((SUBLANES, t), I32), jax.ShapeDtypeStruct((SUBLANES, t), F32),
                   jax.ShapeDtypeStruct((SUBLANES, t), I32), jax.ShapeDtypeStruct((N_EXPERTS, LANES), I32)],
        scratch_shapes=[pltpu.VMEM((N_EXPERTS, LANES), F32)],
        compiler_params=_params(("arbitrary",)),
        name="moe_route",
    )(h, w_router_t, router_bias.reshape(N_EXPERTS, 1), upper)


def _row_copy(src, src_row, dst, dst_row, sem):
    return pltpu.make_async_copy(src.at[pl.ds(src_row, 1)], dst.at[pl.ds(dst_row, 1)], sem)


def _dispatch_kernel(pe_ref, dest_ref, h_ref, xs_out, zbuf, sem, zsem, *, tb):

    @pl.when(pl.program_id(0) == 0)
    def _():
        zbuf[...] = jnp.zeros(zbuf.shape, zbuf.dtype)

        def nonempty(e):
            return pe_ref[e] > jnp.where(e > 0, pe_ref[jnp.maximum(e - 1, 0)], 0)

        def zero_tile(e):
            first = pl.multiple_of(pe_ref[e] - EXPERT_TILE, EXPERT_TILE)
            return pltpu.make_async_copy(zbuf, xs_out.at[pl.ds(first, EXPERT_TILE)], zsem)

        def zstart(e, carry):
            @pl.when(nonempty(e))
            def _():
                zero_tile(e).start()
            return carry

        def zwait(e, carry):
            @pl.when(nonempty(e))
            def _():
                zero_tile(e).wait()
            return carry

        lax.fori_loop(0, N_EXPERTS, zstart, 0)
        lax.fori_loop(0, N_EXPERTS, zwait, 0)

    def start(r, carry):
        for k in range(MOE_TOPK):
            _row_copy(h_ref, r, xs_out, dest_ref[0, k, r], sem).start(priority=k % DMA_PRIORITIES)
        return carry

    lax.fori_loop(0, tb, start, 0, unroll=DMA_UNROLL)
    for k in range(MOE_TOPK):
        pltpu.make_async_copy(h_ref, xs_out.at[pl.ds(0, tb)], sem).wait()


def dispatch(h, dest_tiles, pad_end, n_rows, tb=SCATTER_TILE):
    t, d = h.shape
    assert t % tb == 0
    grid_spec = pltpu.PrefetchScalarGridSpec(
        num_scalar_prefetch=1,
        grid=(t // tb,),
        in_specs=[pl.BlockSpec((1, SUBLANES, tb), lambda i, pe: (i, 0, 0), memory_space=pltpu.SMEM),
                  pl.BlockSpec((tb, d), lambda i, pe: (i, 0))],
        out_specs=pl.BlockSpec(memory_space=pl.ANY),
        scratch_shapes=[pltpu.VMEM((EXPERT_TILE, d), F32), pltpu.SemaphoreType.DMA, pltpu.SemaphoreType.DMA],
    )
    return pl.pallas_call(
        functools.partial(_dispatch_kernel, tb=tb),
        grid_spec=grid_spec,
        out_shape=jax.ShapeDtypeStruct((n_rows, d), F32),
        compiler_params=pltpu.CompilerParams(dimension_semantics=("arbitrary",), has_side_effects=True),
        name="moe_dispatch",
    )(pad_end, dest_tiles, h)


def _col_chunks(n, width=2 * LANES):
    return [(c, min(c + width, n)) for c in range(0, n, width)]


def _expert_up_kernel(te_ref, nu_ref, x_ref, w1_ref, w3_ref, h_ref):
    i = pl.program_id(0)

    @pl.when(i < nu_ref[0])
    def _():
        x = x_ref[...].astype(BF16)
        for c0, c1 in _col_chunks(h_ref.shape[1]):
            a = _dot(x, w1_ref[0, :, c0:c1].astype(BF16))
            b = _dot(x, w3_ref[0, :, c0:c1].astype(BF16))
            h_ref[:, c0:c1] = ((a * jax.nn.sigmoid(a)) * b).astype(h_ref.dtype)

    @pl.when(i >= nu_ref[0])
    def _():
        h_ref[...] = jnp.zeros(h_ref.shape, h_ref.dtype)


def _expert_down_kernel(te_ref, nu_ref, h_ref, w2_ref, o_ref):
    i = pl.program_id(0)

    @pl.when(i < nu_ref[0])
    def _():
        h = h_ref[...]
        for c0, c1 in _col_chunks(o_ref.shape[1]):
            o_ref[:, c0:c1] = _dot(h, w2_ref[0, :, c0:c1].astype(BF16))

    @pl.when(i >= nu_ref[0])
    def _():
        o_ref[...] = jnp.zeros(o_ref.shape, o_ref.dtype)


def expert_mlp_f32(xs, tile_expert, n_used, w1, w3, w2, tm=EXPERT_TILE):
    r, d = xs.shape
    f = w1.shape[2]
    assert r % tm == 0
    nt = r // tm
    rows = lambda i, te, nu: (jnp.minimum(i, nu[0] - 1), 0)
    own = lambda i, te, nu: (i, 0)
    wsel = lambda i, te, nu: (te[jnp.minimum(i, nu[0] - 1)], 0, 0)
    big = pltpu.CompilerParams(dimension_semantics=("arbitrary",), vmem_limit_bytes=60 * 1024 * 1024)
    h = pl.pallas_call(
        _expert_up_kernel,
        grid_spec=pltpu.PrefetchScalarGridSpec(
            num_scalar_prefetch=2, grid=(nt,),
            in_specs=[pl.BlockSpec((tm, d), rows), pl.BlockSpec((1, d, f), wsel), pl.BlockSpec((1, d, f), wsel)],
            out_specs=pl.BlockSpec((tm, f), own)),
        out_shape=jax.ShapeDtypeStruct((r, f), BF16),
        compiler_params=big,
        name="expert_up",
    )(tile_expert, n_used, xs, w1, w3)
    return pl.pallas_call(
        _expert_down_kernel,
        grid_spec=pltpu.PrefetchScalarGridSpec(
            num_scalar_prefetch=2, grid=(nt,),
            in_specs=[pl.BlockSpec((tm, f), rows), pl.BlockSpec((1, f, d), wsel)],
            out_specs=pl.BlockSpec((tm, d), own)),
        out_shape=jax.ShapeDtypeStruct((r, d), F32),
        compiler_params=big,
        name="expert_down",
    )(tile_expert, n_used, h, w2)


def _expert_kernel(te_ref, nu_ref, x_ref, w1_ref, w3_ref, w2_ref, o_ref):
    i = pl.program_id(0)

    @pl.when(i < nu_ref[0])
    def _():
        x = x_ref[...].astype(BF16)
        a = _dot(x, w1_ref[0])
        h = (a * jax.nn.sigmoid(a)) * _dot(x, w3_ref[0])
        o_ref[...] = _dot(h.astype(BF16), w2_ref[0])

    @pl.when(i >= nu_ref[0])
    def _():
        o_ref[...] = jnp.zeros(o_ref.shape, o_ref.dtype)


def expert_mlp(xs, tile_expert, n_used, w1, w3, w2, tm=EXPERT_TILE):
    r, d = xs.shape
    f = w1.shape[2]
    assert r % tm == 0
    nt = r // tm
    rows = lambda i, te, nu: (jnp.minimum(i, nu[0] - 1), 0)
    wsel = lambda i, te, nu: (te[jnp.minimum(i, nu[0] - 1)], 0, 0)
    grid_spec = pltpu.PrefetchScalarGridSpec(
        num_scalar_prefetch=2,
        grid=(nt,),
        in_specs=[pl.BlockSpec((tm, d), rows), pl.BlockSpec((1, d, f), wsel),
                  pl.BlockSpec((1, d, f), wsel), pl.BlockSpec((1, f, d), wsel)],
        out_specs=pl.BlockSpec((tm, d), lambda i, te, nu: (i, 0)),
    )
    return pl.pallas_call(
        _expert_kernel,
        grid_spec=grid_spec,
        out_shape=jax.ShapeDtypeStruct((r, d), F32),
        compiler_params=_params(("arbitrary",)),
        name="expert_mlp",
    )(tile_expert, n_used, xs, w1, w3, w2)


def _combine_kernel(dest_ref, gate_ref, h_ref, sh_ref, g_ref, b_ref, o_hbm, y_ref, buf, sem, *, tb, alpha):

    def start(r, carry):
        for k in range(MOE_TOPK):
            _row_copy(o_hbm, dest_ref[0, k, r], buf.at[k], r, sem).start(priority=k % DMA_PRIORITIES)
        return carry

    lax.fori_loop(0, tb, start, 0, unroll=DMA_UNROLL)
    for k in range(MOE_TOPK):
        pltpu.make_async_copy(o_hbm.at[pl.ds(0, tb)], buf.at[k], sem).wait()
    gate = gate_ref[...]
    routed = gate[:, 0:1] * buf[0]
    for k in range(1, MOE_TOPK):
        routed = routed + gate[:, k:k + 1] * buf[k]
    z = alpha * h_ref[...] + (routed + sh_ref[...])
    y_ref[...] = _layer_norm(z, g_ref[...], b_ref[...])


def combine_ln(dest_tiles, gate_t, h, shared, o_sorted, g, b, alpha, tb=COMBINE_TILE):
    t, d = h.shape
    assert t % tb == 0
    row = lambda i: (i, 0)
    fixed = lambda i: (0, 0)
    return pl.pallas_call(
        functools.partial(_combine_kernel, tb=tb, alpha=alpha),
        grid=(t // tb,),
        in_specs=[pl.BlockSpec((1, SUBLANES, tb), lambda i: (i, 0, 0), memory_space=pltpu.SMEM),
                  pl.BlockSpec((tb, SUBLANES), row), pl.BlockSpec((tb, d), row), pl.BlockSpec((tb, d), row),
                  pl.BlockSpec((1, d), fixed), pl.BlockSpec((1, d), fixed),
                  pl.BlockSpec(memory_space=pl.ANY)],
        out_specs=pl.BlockSpec((tb, d), row),
        out_shape=jax.ShapeDtypeStruct((t, d), F32),
        scratch_shapes=[pltpu.VMEM((MOE_TOPK, tb, d), F32), pltpu.SemaphoreType.DMA],
        compiler_params=_params(("arbitrary",)),
        name="moe_combine",
    )(dest_tiles, gate_t, h, shared, g, b, o_sorted)


def moe_ffn_ln(h1, w_router, router_bias, w1, w3, w2, ws1, ws3, ws2, g, b, alpha):
    t, d = h1.shape
    idx, gate, rank, cnt = route(h1, w_router.T, router_bias)
    counts = cnt[:, 0]
    padded = (counts + EXPERT_TILE - 1) // EXPERT_TILE * EXPERT_TILE
    e_ids = jnp.arange(N_EXPERTS, dtype=I32)
    pad_end = jnp.sum(jnp.where(e_ids[None, :] <= e_ids[:, None], padded[None, :], 0), axis=1)
    pad_start = pad_end - padded
    n_tiles = -(-(t * MOE_TOPK) // EXPERT_TILE) + N_EXPERTS
    tile_start = jnp.arange(n_tiles, dtype=I32) * EXPERT_TILE
    tile_expert = jnp.minimum(jnp.sum((pad_end[None, :] <= tile_start[:, None]).astype(I32), axis=1),
                              N_EXPERTS - 1)
    n_used = (pad_end[-1] // EXPERT_TILE).astype(I32).reshape(1)
    dest = jnp.sum(jnp.where(idx[:, :, None] == e_ids, pad_start, 0), axis=-1) + rank

    def tiles(a, tb):
        return a.reshape(SUBLANES, t // tb, tb).transpose(1, 0, 2)

    xs = dispatch(h1, tiles(dest, SCATTER_TILE), pad_end.astype(I32), n_tiles * EXPERT_TILE)
    o_sorted = expert_mlp_f32(xs, tile_expert, n_used, w1, w3, w2)
    n_dense = t // ROUTE_TILE
    shared = expert_mlp_f32(h1, jnp.zeros((n_dense,), I32), jnp.full((1,), n_dense, I32),
                            ws1[None], ws3[None], ws2[None], tm=ROUTE_TILE)
    return combine_ln(tiles(dest, COMBINE_TILE), gate.T, h1, shared, o_sorted, g, b, alpha)


def _split_w_in(w_in, d_conv):
    d = w_in.shape[0]
    hd = N_HEADS * HEAD_DIM
    kvw = 2 * N_KV * HEAD_DIM
    o = 0
    parts = {}
    for name, width in (("b", d_conv), ("c", d_conv), ("h", d_conv), ("q", hd), ("cmp", kvw), ("sel", kvw),
                        ("win", kvw), ("nsa", 3 * N_HEADS), ("merge", 2 * d)):
        parts[name] = w_in[:, o:o + width]
        o += width
    nsa = parts["nsa"].reshape(d, N_KV, HPG * 3)
    parts["nsa"] = jnp.pad(nsa, ((0, 0), (0, 0), (0, LANES - HPG * 3))).reshape(d, N_KV * LANES)
    return {k: v.astype(BF16) for k, v in parts.items()}


def kernel(x_prompt, x_sample, cache_cmp_kv, cache_sel_kv, state_win_kv, state_conv, page_table,
           w_in, conv_w, w_phi_k, w_phi_v, w_conv_out, w_attn_out, w_o, ln1_g, ln1_b,
           w_router, router_bias, w_e_gate, w_e_up, w_e_down, w_s_gate, w_s_up, w_s_down,
           ln2_g, ln2_b):
    depth = w_in.shape[0]
    assert depth == 1
    alpha = (2.0 * depth) ** 0.25
    bp, tp, d = x_prompt.shape
    ns_, ts, _ = x_sample.shape
    assert ts == 1
    n_pages = page_table.shape[1]
    past_len = n_pages * PAGE_SIZE
    d_conv = conv_w.shape[2]
    kvw = 2 * N_KV * HEAD_DIM
    gw = N_KV * HEAD_DIM
    hd = N_HEADS * HEAD_DIM
    t_all = bp * tp + ns_

    w = _split_w_in(w_in[0], d_conv)
    cw = conv_w[0]
    wk_phi = w_phi_k[0].reshape(CMP_BLOCK * HEAD_DIM, HEAD_DIM).astype(BF16)
    wv_phi = w_phi_v[0].reshape(CMP_BLOCK * HEAD_DIM, HEAD_DIM).astype(BF16)
    wco = w_conv_out[0].astype(BF16)
    wao = w_attn_out[0].astype(BF16)
    wo = w_o[0].astype(BF16)
    g1, b1 = ln1_g[0].reshape(1, d), ln1_b[0].reshape(1, d)
    g2, b2 = ln2_g[0].reshape(1, d), ln2_b[0].reshape(1, d)

    xpb = x_prompt.astype(BF16)
    xp2 = xpb.reshape(bp * tp, d)
    ya_p, st_p = conv_proj(xpb, w["b"], w["c"], w["h"], cw, jnp.zeros((bp, SUBLANES, d_conv), F32))
    (q_p,) = matmul(xp2, w["q"], (BF16,))
    (cmp_p,) = kv_proj(xp2, w["cmp"], False)
    sel_p, sel_pb = kv_proj(xp2, w["sel"], True)
    win_p, win_pb = kv_proj(xp2, w["win"], True)
    (ng_p,) = matmul(xp2, w["nsa"], (F32,))
    nc_p = tp // CMP_BLOCK
    slabs = lambda a: a.reshape(-1, HEAD_DIM)
    kc_p, vc_p = cmp_kv(cmp_p, wk_phi, wv_phi, nb=nc_p)
    pad_c = lambda a: jnp.pad(a, ((0, 0), (0, 0), (0, LANES - a.shape[2]), (0, 0))).astype(BF16)
    blk_of_key = jnp.arange(tp, dtype=I32) // SEL_BLOCK
    expand = (jnp.arange(LANES, dtype=I32)[:, None] == 2 * blk_of_key[None, :]).astype(BF16)
    o_p = prompt_attention(q_p.reshape(bp, tp, hd), ng_p.reshape(bp, tp, N_KV * LANES), pad_c(kc_p), pad_c(vc_p),
                           sel_pb.reshape(bp, tp, kvw), win_pb.reshape(bp, tp, kvw), expand)
    mix_p = merge_mixers(xp2, ya_p.reshape(bp * tp, d_conv), o_p.reshape(bp * tp, hd), w["merge"], wco, wao)
    h1 = out_proj_ln(mix_p, wo, x_prompt.reshape(bp * tp, d), g1, b1, alpha, t_all, 0,
                     prev=jnp.zeros((t_all, d), F32))

    xs2 = x_sample.reshape(ns_, d)
    xsb = xs2.astype(BF16)
    w_conv3 = jnp.concatenate([w["b"], w["c"], w["h"]], axis=1)
    (pconv_s,) = matmul(xsb, w_conv3, (F32,))
    ya_s, u_s = step_conv(pconv_s, state_conv[0, :, 0], state_conv[0, :, 1], cw)
    (q_s,) = matmul(xsb, w["q"], (BF16,))
    (cmp_s,) = matmul(xsb, w["cmp"], (F32,))
    (sel_s,) = matmul(xsb, w["sel"], (F32,))
    (win_s,) = matmul(xsb, w["win"], (F32,))
    (ng_s,) = matmul(xsb, w["nsa"], (F32,))
    n_phys = cache_cmp_kv.shape[1]
    bpp = PAGE_SIZE // CMP_BLOCK
    pages_per_tile = 32
    assert n_phys % pages_per_tile == 0
    kc_all, vc_all = cmp_kv(slabs(cache_cmp_kv), wk_phi, wv_phi, nb=pages_per_tile * bpp)

    def per_sequence(a):
        a = a.reshape(n_phys // pages_per_tile, N_KV, pages_per_tile, bpp, HEAD_DIM)
        a = a.transpose(0, 2, 3, 1, 4).reshape(n_phys, bpp, gw)
        a = a[page_table].reshape(ns_, n_pages * bpp, gw)
        return jnp.pad(a, ((0, 0), (0, LANES - n_pages * bpp), (0, 0))).astype(BF16)

    head_group = jnp.arange(LANES, dtype=I32) // HPG
    qh = jnp.pad(q_s.reshape(ns_, N_HEADS, HEAD_DIM), ((0, 0), (0, LANES - N_HEADS), (0, 0)))
    qblk = jnp.where(jnp.arange(N_KV, dtype=I32)[None, :, None, None] == head_group[None, None, None, :],
                     qh.transpose(0, 2, 1)[:, None], 0).reshape(ns_, gw, LANES).astype(BF16)
    nk = n_pages * PAGE_SIZE
    key_row = jnp.arange(nk + 2 * SUBLANES, dtype=I32)
    expand_t = ((2 * (key_row // SEL_BLOCK))[:, None] == jnp.arange(LANES, dtype=I32)[None, :]) & (key_row <= nk)[:, None]
    o_s, win_next = step_attention(qblk, ng_s.reshape(ns_, 1, N_KV * LANES), per_sequence(kc_all), per_sequence(vc_all),
                         slabs(cache_sel_kv), page_table, sel_s.reshape(ns_, 1, kvw), slabs(state_win_kv),
                         win_s.reshape(ns_, 1, kvw), expand_t.astype(BF16), past_len)
    mix_s = merge_mixers(xsb, ya_s, o_s.reshape(ns_, hd), w["merge"], wco, wao)
    h1 = out_proj_ln(mix_s, wo, xs2, g1, b1, alpha, t_all, bp * tp, prev=h1)

    y = moe_ffn_ln(h1, w_router[0], router_bias[0], w_e_gate[0], w_e_up[0], w_e_down[0],
                   w_s_gate[0], w_s_up[0], w_s_down[0], g2, b2, alpha)

    kv6 = lambda a, n, t: a.reshape(1, n, t, 2, N_KV, HEAD_DIM)
    win_keep = min(WINDOW, tp)
    new_win_s = win_next.reshape(state_win_kv.shape)
    new_conv_s = jnp.stack([state_conv[0, :, 1], u_s], axis=1)[None]
    return (y[:bp * tp].reshape(bp, tp, d), y[bp * tp:].reshape(ns_, ts, d),
            kv6(cmp_p, bp, tp), kv6(sel_p, bp, tp), kv6(win_p, bp, tp)[:, :, tp - win_keep:],
            st_p[None, :, SUBLANES - (CONV_WIDTH - 1):],
            kv6(cmp_s, ns_, 1), kv6(sel_s, ns_, 1), new_win_s, new_conv_s)
```

```python
import functools

import jax
import jax.numpy as jnp
from jax import lax
from jax.experimental import pallas as pl
from jax.experimental.pallas import tpu as pltpu

F32 = jnp.float32
BF16 = jnp.bfloat16
I32 = jnp.int32

N_HEADS = 16
HEAD_DIM = 128
N_KV = 4
HPG = N_HEADS // N_KV
CMP_BLOCK = 32
SEL_BLOCK = 64
SEL_TOPK = 8
WINDOW = 512
PAGE_SIZE = 128
CONV_WIDTH = 3
N_EXPERTS = 64
MOE_TOPK = 6
N_GROUPS = 8
TOPK_GROUPS = 4
ROUTED_SCALE = 2.5
LN_EPS = 1e-5
NEG_INF = -1e30
FORCE_SCORE = 1e4
ATTN_SCALE = HEAD_DIM ** -0.5
LOG2_E = 1.4426950408889634

LANES = 128
SUBLANES = 8
VMEM_LIMIT = 56 * 1024 * 1024

EXPERT_TILE = 256
ROUTE_TILE = 384
COMBINE_TILE = 64
SCATTER_TILE = 128
DMA_UNROLL = 8
DMA_PRIORITIES = 2


def _dot(a, b):
    return jnp.dot(a, b, preferred_element_type=F32)


def _dot_nt(a, b):
    return lax.dot_general(a, b, (((1,), (1,)), ((), ())), preferred_element_type=F32)


def _dot_tn(a, b):
    return lax.dot_general(a, b, (((0,), (0,)), ((), ())), preferred_element_type=F32)


def _params(sem):
    return pltpu.CompilerParams(dimension_semantics=sem, vmem_limit_bytes=VMEM_LIMIT)


def _mm_kernel(x_ref, w_ref, *o_refs):
    r = _dot(x_ref[...], w_ref[...])
    for o_ref in o_refs:
        o_ref[...] = r.astype(o_ref.dtype)


def matmul(x, w, out_dtypes, tm=1024, tn=1024):
    m, k = x.shape
    n = w.shape[1]
    tm = min(tm, m)
    tn = min(tn, n)
    assert m % tm == 0 and n % tn == 0
    outs = pl.pallas_call(
        _mm_kernel,
        grid=(n // tn, m // tm),
        in_specs=[pl.BlockSpec((tm, k), lambda j, i: (i, 0)),
                  pl.BlockSpec((k, tn), lambda j, i: (0, j))],
        out_specs=[pl.BlockSpec((tm, tn), lambda j, i: (i, j)) for _ in out_dtypes],
        out_shape=[jax.ShapeDtypeStruct((m, n), dt) for dt in out_dtypes],
        compiler_params=_params(("parallel", "parallel")),
        name="proj_mm",
    )(x, w)
    return outs


def _kv_proj_kernel(x_ref, w_ref, slab_ref, *b_refs):
    r = _dot(x_ref[...], w_ref[...])
    tm, n = r.shape
    nslab = n // HEAD_DIM
    for c in range(nslab):
        slab_ref[pl.ds(c, tm, stride=nslab), :] = r[:, c * HEAD_DIM:(c + 1) * HEAD_DIM]
    for b_ref in b_refs:
        b_ref[...] = r.astype(b_ref.dtype)


def kv_proj(x, w, with_bf16, tm=1024):
    m, k = x.shape
    n = w.shape[1]
    nslab = n // HEAD_DIM
    tm = min(tm, m)
    assert m % tm == 0
    out_specs = [pl.BlockSpec((tm * nslab, HEAD_DIM), lambda i: (i, 0))]
    out_shape = [jax.ShapeDtypeStruct((m * nslab, HEAD_DIM), F32)]
    if with_bf16:
        out_specs.append(pl.BlockSpec((tm, n), lambda i: (i, 0)))
        out_shape.append(jax.ShapeDtypeStruct((m, n), BF16))
    return pl.pallas_call(
        _kv_proj_kernel,
        grid=(m // tm,),
        in_specs=[pl.BlockSpec((tm, k), lambda i: (i, 0)), pl.BlockSpec((k, n), lambda i: (0, 0))],
        out_specs=out_specs,
        out_shape=out_shape,
        compiler_params=_params(("parallel",)),
        name="kv_proj",
    )(x, w)


def _conv_proj_kernel(x_ref, wb_ref, wc_ref, wh_ref, cw_ref, pre_ref, ya_ref, st_ref, s_ref, *, tm):
    i = pl.program_id(2)

    @pl.when(i == 0)
    def _():
        s_ref[0:SUBLANES, :] = pre_ref[0]

    x = x_ref[0]
    pb = _dot(x, wb_ref[...])
    u = _dot(x, wc_ref[...]) * _dot(x, wh_ref[...])
    s_ref[SUBLANES:SUBLANES + tm, :] = u
    u1 = s_ref[SUBLANES - 1:SUBLANES - 1 + tm, :]
    u2 = s_ref[SUBLANES - 2:SUBLANES - 2 + tm, :]
    cw = cw_ref[...]
    y = cw[0:1] * u2 + cw[1:2] * u1 + cw[2:3] * u
    ya_ref[0] = (pb * y).astype(ya_ref.dtype)
    last = s_ref[tm:tm + SUBLANES, :]
    s_ref[0:SUBLANES, :] = last
    st_ref[0] = last


def conv_proj(xb, wb, wc, wh, conv_w, prefix8, tm=512, tn=512):
    b, t, d = xb.shape
    dc = wb.shape[1]
    tm = min(tm, t)
    assert t % tm == 0 and dc % tn == 0 and tm >= SUBLANES
    return pl.pallas_call(
        functools.partial(_conv_proj_kernel, tm=tm),
        grid=(dc // tn, b, t // tm),
        in_specs=[pl.BlockSpec((1, tm, d), lambda j, bb, i: (bb, i, 0)),
                  pl.BlockSpec((d, tn), lambda j, bb, i: (0, j)),
                  pl.BlockSpec((d, tn), lambda j, bb, i: (0, j)),
                  pl.BlockSpec((d, tn), lambda j, bb, i: (0, j)),
                  pl.BlockSpec((CONV_WIDTH, tn), lambda j, bb, i: (0, j)),
                  pl.BlockSpec((1, SUBLANES, tn), lambda j, bb, i: (bb, 0, j))],
        out_specs=[pl.BlockSpec((1, tm, tn), lambda j, bb, i: (bb, i, j)),
                   pl.BlockSpec((1, SUBLANES, tn), lambda j, bb, i: (bb, 0, j))],
        out_shape=[jax.ShapeDtypeStruct((b, t, dc), BF16),
                   jax.ShapeDtypeStruct((b, SUBLANES, dc), F32)],
        scratch_shapes=[pltpu.VMEM((tm + SUBLANES, tn), F32)],
        compiler_params=_params(("parallel", "parallel", "arbitrary")),
        name="conv_proj",
    )(xb, wb, wc, wh, conv_w, prefix8)


def _step_conv_kernel(p_ref, s0_ref, s1_ref, cw_ref, ya_ref, u_ref, *, dc):
    pb = p_ref[:, 0:dc]
    u = p_ref[:, dc:2 * dc] * p_ref[:, 2 * dc:3 * dc]
    cw = cw_ref[...]
    y = cw[0:1] * s0_ref[...] + cw[1:2] * s1_ref[...] + cw[2:3] * u
    ya_ref[...] = (pb * y).astype(ya_ref.dtype)
    u_ref[...] = u


def step_conv(p, s0, s1, conv_w):
    n, dc3 = p.shape
    dc = dc3 // 3
    return pl.pallas_call(
        functools.partial(_step_conv_kernel, dc=dc),
        out_shape=[jax.ShapeDtypeStruct((n, dc), BF16), jax.ShapeDtypeStruct((n, dc), F32)],
        compiler_params=pltpu.CompilerParams(vmem_limit_bytes=VMEM_LIMIT),
        name="step_conv",
    )(p, s0, s1, conv_w)


def _cmp_kv_kernel(x_ref, wk_ref, wv_ref, ok_ref, ov_ref, *, nb):
    nslab = 2 * N_KV

    def rows(l, c):
        return x_ref[pl.ds(l * nslab + c, nb, stride=CMP_BLOCK * nslab), :].astype(BF16)

    for kv, (w_ref, o_ref) in enumerate(((wk_ref, ok_ref), (wv_ref, ov_ref))):
        per_group = [jnp.concatenate([rows(l, kv * N_KV + g) for l in range(CMP_BLOCK)], axis=1)
                     for g in range(N_KV)]
        lhs = jnp.concatenate(per_group, axis=0)
        r = _dot(lhs, w_ref[...])
        for g in range(N_KV):
            o_ref[0, g] = r[g * nb:(g + 1) * nb]


def cmp_kv(raw, wk, wv, nb):
    n_rows, width = raw.shape
    rows_per_step = nb * CMP_BLOCK * 2 * N_KV
    assert n_rows % rows_per_step == 0 and width == HEAD_DIM
    nt = n_rows // rows_per_step
    return pl.pallas_call(
        functools.partial(_cmp_kv_kernel, nb=nb),
        grid=(nt,),
        in_specs=[pl.BlockSpec((rows_per_step, width), lambda i: (i, 0)),
                  pl.BlockSpec(wk.shape, lambda i: (0, 0)),
                  pl.BlockSpec(wv.shape, lambda i: (0, 0))],
        out_specs=[pl.BlockSpec((1, N_KV, nb, HEAD_DIM), lambda i: (i, 0, 0, 0))] * 2,
        out_shape=[jax.ShapeDtypeStruct((nt, N_KV, nb, HEAD_DIM), F32)] * 2,
        compiler_params=_params(("parallel",)),
        name="cmp_kv",
    )(raw, wk, wv)


def _top_k_mask(val, iota, axis, k, size):
    sel = jnp.zeros(val.shape, F32)
    iota = iota.astype(F32)
    for _ in range(k):
        m = jnp.max(val, axis=axis, keepdims=True)
        first = jnp.min(jnp.where(val == m, iota, float(size)), axis=axis, keepdims=True)
        hit = iota == first
        sel = jnp.where(hit, 1.0, sel)
        val = jnp.where(hit, -jnp.inf, val)
    return sel


def _prompt_attn_kernel(q_ref, ng_ref, kc_ref, vc_ref, ks_ref, vs_ref, kw_ref, vw_ref, e_ref, o_ref,
                        m_ref, l_ref, acc_ref, oacc_ref, *, tq, tks, tkw, nc, ns):
    qt = pl.program_id(2)
    t0 = qt * tq
    qpos = t0 + lax.broadcasted_iota(I32, (tq, 1), 0)
    gates = jax.nn.sigmoid(ng_ref[0])

    qcol = t0 + lax.broadcasted_iota(I32, (1, tq), 1)
    rowb = lax.broadcasted_iota(I32, (LANES, tq), 0)
    vis = ((rowb + 1) * CMP_BLOCK - 1 <= qcol) & (rowb < nc)
    visf = vis.astype(F32)
    kc = kc_ref[0, 0]
    vc = vc_ref[0, 0]
    imp = jnp.zeros((LANES, tq), F32)
    for h in range(HPG):
        qh = q_ref[0, :, h * HEAD_DIM:(h + 1) * HEAD_DIM]
        s = jnp.where(vis, _dot_nt(kc, qh) * ATTN_SCALE, NEG_INF)
        e = jnp.exp(s - jnp.max(s, axis=0, keepdims=True))
        p = e / jnp.sum(e, axis=0, keepdims=True) * visf
        imp = imp + p
        oacc_ref[h] = gates[:, h * 3:h * 3 + 1] * _dot_tn(p.astype(BF16), vc)

    pair = imp + pltpu.roll(imp, LANES - 1, axis=0)
    blk = rowb >> 1
    cand = ((rowb & 1) == 0) & (blk < ns)
    cur = qcol >> (SEL_BLOCK.bit_length() - 1)
    forced = (blk == 0) | (blk == cur) | (blk == cur - 1)
    val = jnp.where(blk * SEL_BLOCK <= qcol, jnp.where(forced, FORCE_SCORE, pair), -1.0)
    val = jnp.where(cand, val, -jnp.inf)
    selb = _top_k_mask(val, rowb, 0, SEL_TOPK, LANES).astype(BF16)

    def flash(k_ref, v_ref, tk, kt_lo, kt_hi, mask_fn, branch):
        m_ref[...] = jnp.full(m_ref.shape, NEG_INF, F32)
        l_ref[...] = jnp.zeros(l_ref.shape, F32)
        acc_ref[...] = jnp.zeros(acc_ref.shape, F32)

        def body(kt, carry):
            k0 = pl.multiple_of(kt * tk, tk)
            k = k_ref[0, pl.ds(k0, tk), :]
            v = v_ref[0, pl.ds(k0, tk), :]
            mask = mask_fn(k0, tk)
            for h in range(HPG):
                qh = q_ref[0, :, h * HEAD_DIM:(h + 1) * HEAD_DIM]
                s = jnp.where(mask, _dot_nt(qh, k) * (ATTN_SCALE * LOG2_E), NEG_INF)
                m_prev = m_ref[h]
                m_next = jnp.maximum(m_prev, jnp.max(s, axis=-1, keepdims=True))
                alpha = jnp.exp2(m_prev - m_next)
                p = jnp.exp2(s - jnp.tile(m_next, (1, tk // LANES)))
                psum = p[:, 0:LANES]
                for c in range(1, tk // LANES):
                    psum = psum + p[:, c * LANES:(c + 1) * LANES]
                l_ref[h] = alpha * l_ref[h] + psum
                acc_ref[h] = alpha * acc_ref[h] + _dot(p.astype(BF16), v)
                m_ref[h] = m_next
            return carry

        lax.fori_loop(kt_lo, kt_hi, body, 0)
        for h in range(HPG):
            c = h * 3 + branch
            l_tot = jnp.sum(l_ref[h], axis=-1, keepdims=True)
            oacc_ref[h] = oacc_ref[h] + gates[:, c:c + 1] * (acc_ref[h] / l_tot)

    def sel_mask(k0, tk):
        kpos = k0 + lax.broadcasted_iota(I32, (1, tk), 1)
        chosen = _dot_tn(selb, e_ref[:, pl.ds(k0, tk)]) > 0.5
        return chosen & (kpos <= qpos)

    def win_mask(k0, tk):
        kpos = k0 + lax.broadcasted_iota(I32, (1, tk), 1)
        return (kpos <= qpos) & (kpos >= qpos - WINDOW)

    last = t0 + tq - 1
    flash(ks_ref, vs_ref, tks, 0, last // tks + 1, sel_mask, 1)
    flash(kw_ref, vw_ref, tkw, jnp.maximum(t0 - WINDOW, 0) // tkw, last // tkw + 1, win_mask, 2)
    for h in range(HPG):
        o_ref[0, :, h * HEAD_DIM:(h + 1) * HEAD_DIM] = oacc_ref[h].astype(o_ref.dtype)


def prompt_attention(q, ng, kc, vc, sel_b, win_b, expand, tq=512, tks=256, tkw=256):
    b, t, _ = q.shape
    tks = min(tks, t)
    assert t % tq == 0 and t % tks == 0 and t % tkw == 0
    nc = t // CMP_BLOCK
    ns = max(-(-t // SEL_BLOCK), SEL_TOPK)
    assert nc <= LANES and 2 * ns <= LANES and ns * SEL_BLOCK == t
    gw = HPG * HEAD_DIM
    kern = functools.partial(_prompt_attn_kernel, tq=tq, tks=tks, tkw=tkw, nc=nc, ns=ns)
    kv_spec = lambda off: pl.BlockSpec((1, t, HEAD_DIM), lambda bb, g, i: (bb, 0, off + g))
    return pl.pallas_call(
        kern,
        grid=(b, N_KV, t // tq),
        in_specs=[pl.BlockSpec((1, tq, gw), lambda bb, g, i: (bb, i, g)),
                  pl.BlockSpec((1, tq, LANES), lambda bb, g, i: (bb, i, g)),
                  pl.BlockSpec((1, 1, LANES, HEAD_DIM), lambda bb, g, i: (bb, g, 0, 0)),
                  pl.BlockSpec((1, 1, LANES, HEAD_DIM), lambda bb, g, i: (bb, g, 0, 0)),
                  kv_spec(0), kv_spec(N_KV), kv_spec(0), kv_spec(N_KV),
                  pl.BlockSpec((LANES, t), lambda bb, g, i: (0, 0))],
        out_specs=pl.BlockSpec((1, tq, gw), lambda bb, g, i: (bb, i, g)),
        out_shape=jax.ShapeDtypeStruct(q.shape, BF16),
        scratch_shapes=[pltpu.VMEM((HPG, tq, LANES), F32), pltpu.VMEM((HPG, tq, LANES), F32),
                        pltpu.VMEM((HPG, tq, HEAD_DIM), F32), pltpu.VMEM((HPG, tq, HEAD_DIM), F32)],
        compiler_params=_params(("parallel", "parallel", "arbitrary")),
        name="prompt_attn",
    )(q, ng, kc, vc, sel_b, sel_b, win_b, win_b, expand)


def _step_attn_kernel(*refs, n_pages, past_len, nc, ns):
    (_, qb_ref, ng_ref, kc_ref, vc_ref) = refs[:5]
    pages = refs[5:5 + n_pages]
    (snew_ref, win_ref, wnew_ref, wslab_ref, et_ref, o_ref, wout_ref, kbuf, vbuf, kwbuf, vwbuf) = refs[5 + n_pages:]
    gw = N_KV * HEAD_DIM
    qb = qb_ref[0]
    lane = lax.broadcasted_iota(I32, (LANES, LANES), 1)
    rowi = lax.broadcasted_iota(I32, (LANES, LANES), 0)

    def attend(k, v, mask):
        s = jnp.where(mask, _dot(k, qb) * ATTN_SCALE, NEG_INF)
        e = jnp.exp(s - jnp.max(s, axis=0, keepdims=True))
        p = e / jnp.sum(e, axis=0, keepdims=True) * mask.astype(F32)
        return p, _dot_tn(p.astype(BF16), v)

    vis = ((rowi + 1) * CMP_BLOCK - 1 <= past_len) & (rowi < nc)
    pc, oc = attend(kc_ref[0], vc_ref[0], vis)

    a = pc + pltpu.roll(pc, LANES - 1, axis=1)
    a = a + pltpu.roll(a, LANES - 2, axis=1)
    pair = a + pltpu.roll(a, LANES - 1, axis=0)
    blk = rowi >> 1
    cand = ((rowi & 1) == 0) & (blk < ns)
    cur = past_len // SEL_BLOCK
    forced = (blk == 0) | (blk == cur) | (blk == cur - 1)
    val = jnp.where(blk * SEL_BLOCK <= past_len, jnp.where(forced, FORCE_SCORE, pair), -1.0)
    val = jnp.where(cand, val, -jnp.inf)
    sel = _top_k_mask(val, rowi, 0, SEL_TOPK, LANES)
    sel = jnp.where(((lane & (HPG - 1)) == 0) & (lane < N_HEADS), sel, 0.0)
    selh = sel
    for r in range(1, HPG):
        selh = selh + pltpu.roll(sel, r, axis=1)

    nk = n_pages * PAGE_SIZE
    nslab = 2 * N_KV
    tail = lax.broadcasted_iota(I32, (2 * SUBLANES, gw), 0) == 0
    for p in range(n_pages):
        for g in range(N_KV):
            cols = slice(g * HEAD_DIM, (g + 1) * HEAD_DIM)
            kbuf[p * PAGE_SIZE:(p + 1) * PAGE_SIZE, cols] = pages[p][pl.ds(g, PAGE_SIZE, stride=nslab), :].astype(BF16)
            vbuf[p * PAGE_SIZE:(p + 1) * PAGE_SIZE, cols] = pages[p][pl.ds(N_KV + g, PAGE_SIZE, stride=nslab), :].astype(BF16)
    kbuf[nk:nk + 2 * SUBLANES, :] = jnp.where(tail, snew_ref[0, :, 0:gw], 0.0).astype(BF16)
    vbuf[nk:nk + 2 * SUBLANES, :] = jnp.where(tail, snew_ref[0, :, gw:2 * gw], 0.0).astype(BF16)
    chosen = _dot(et_ref[...], selh.astype(BF16)) > 0.5
    _, osel = attend(kbuf[...], vbuf[...], chosen)

    nw = win_ref.shape[0] // nslab
    for g in range(N_KV):
        cols = slice(g * HEAD_DIM, (g + 1) * HEAD_DIM)
        kwbuf[0:nw, cols] = win_ref[pl.ds(g, nw, stride=nslab), :].astype(BF16)
        vwbuf[0:nw, cols] = win_ref[pl.ds(N_KV + g, nw, stride=nslab), :].astype(BF16)
    kwbuf[nw:nw + 2 * SUBLANES, :] = jnp.where(tail, wnew_ref[0, :, 0:gw], 0.0).astype(BF16)
    vwbuf[nw:nw + 2 * SUBLANES, :] = jnp.where(tail, wnew_ref[0, :, gw:2 * gw], 0.0).astype(BF16)
    wrow = lax.broadcasted_iota(I32, (nw + 2 * SUBLANES, LANES), 0)
    _, owin = attend(kwbuf[...], vwbuf[...], wrow <= nw)
    wout_ref[0:(nw - 1) * nslab, :] = win_ref[nslab:nw * nslab, :]
    wout_ref[(nw - 1) * nslab:nw * nslab, :] = wslab_ref[...]

    gates = jax.nn.sigmoid(ng_ref[0])
    for h in range(N_HEADS):
        g, hh = divmod(h, HPG)
        c = g * LANES + hh * 3
        cols = slice(g * HEAD_DIM, (g + 1) * HEAD_DIM)
        o_h = (gates[:, c:c + 1] * oc[h:h + 1, cols] + gates[:, c + 1:c + 2] * osel[h:h + 1, cols]
               + gates[:, c + 2:c + 3] * owin[h:h + 1, cols])
        o_ref[0, :, h * HEAD_DIM:(h + 1) * HEAD_DIM] = o_h.astype(o_ref.dtype)


def step_attention(qblk, ng, kc, vc, sel_pool, page_table, sel_new, win_state, win_new, expand_t, past_len):
    n, n_pages = page_table.shape
    gw = N_KV * HEAD_DIM
    nslab = 2 * N_KV
    nk = n_pages * PAGE_SIZE
    nw = win_state.shape[0] // (n * nslab)
    nc = (past_len + 1) // CMP_BLOCK
    ns = max(-(-(past_len + 1) // SEL_BLOCK), SEL_TOPK)
    assert nc <= LANES and 2 * ns <= LANES and nw == WINDOW and past_len >= WINDOW
    kern = functools.partial(_step_attn_kernel, n_pages=n_pages, past_len=past_len, nc=nc, ns=ns)
    page_specs = [pl.BlockSpec((PAGE_SIZE * nslab, HEAD_DIM), functools.partial(lambda i, pt, p: (pt[i, p], 0), p=p))
                  for p in range(n_pages)]
    grid_spec = pltpu.PrefetchScalarGridSpec(
        num_scalar_prefetch=1,
        grid=(n,),
        in_specs=[pl.BlockSpec((1, gw, LANES), lambda i, pt: (i, 0, 0)),
                  pl.BlockSpec((1, 1, N_KV * LANES), lambda i, pt: (i, 0, 0)),
                  pl.BlockSpec((1, LANES, gw), lambda i, pt: (i, 0, 0)),
                  pl.BlockSpec((1, LANES, gw), lambda i, pt: (i, 0, 0))]
        + page_specs
        + [pl.BlockSpec((1, 1, 2 * gw), lambda i, pt: (i, 0, 0)),
           pl.BlockSpec((nw * nslab, HEAD_DIM), lambda i, pt: (i, 0)),
           pl.BlockSpec((1, 1, 2 * gw), lambda i, pt: (i, 0, 0)),
           pl.BlockSpec((nslab, HEAD_DIM), lambda i, pt: (i, 0)),
           pl.BlockSpec((nk + 2 * SUBLANES, LANES), lambda i, pt: (0, 0))],
        out_specs=[pl.BlockSpec((1, 1, N_HEADS * HEAD_DIM), lambda i, pt: (i, 0, 0)),
                   pl.BlockSpec((nw * nslab, HEAD_DIM), lambda i, pt: (i, 0))],
        scratch_shapes=[pltpu.VMEM((nk + 2 * SUBLANES, gw), BF16), pltpu.VMEM((nk + 2 * SUBLANES, gw), BF16),
                        pltpu.VMEM((nw + 2 * SUBLANES, gw), BF16), pltpu.VMEM((nw + 2 * SUBLANES, gw), BF16)],
    )
    return pl.pallas_call(
        kern,
        grid_spec=grid_spec,
        out_shape=[jax.ShapeDtypeStruct((n, 1, N_HEADS * HEAD_DIM), BF16),
                   jax.ShapeDtypeStruct(win_state.shape, F32)],
        compiler_params=_params(("arbitrary",)),
        name="step_attn",
    )(page_table, qblk, ng, kc, vc, *([sel_pool] * n_pages), sel_new, win_state, win_new,
      win_new.reshape(n * nslab, HEAD_DIM), expand_t)


def _merge_kernel(x_ref, ya_ref, ob_ref, wga_ref, wgb_ref, wco_ref, wao_ref, o_ref):
    x = x_ref[...]
    ga = jax.nn.sigmoid(_dot(x, wga_ref[...]))
    gb = jax.nn.sigmoid(_dot(x, wgb_ref[...]))
    y_a = _dot(ya_ref[...], wco_ref[...])
    y_b = _dot(ob_ref[...], wao_ref[...])
    o_ref[...] = (ga * y_a + gb * y_b).astype(o_ref.dtype)


def merge_mixers(xb, ya_in, ob, w_merge, w_conv_out, w_attn_out, tm=512, tn=512):
    m, d = xb.shape
    tm = min(tm, m)
    assert m % tm == 0 and d % tn == 0
    nj = d // tn
    row = lambda j, i: (i, 0)
    col = lambda j, i: (0, j)
    return pl.pallas_call(
        _merge_kernel,
        grid=(nj, m // tm),
        in_specs=[pl.BlockSpec((tm, d), row), pl.BlockSpec((tm, ya_in.shape[1]), row),
                  pl.BlockSpec((tm, ob.shape[1]), row),
                  pl.BlockSpec((d, tn), col), pl.BlockSpec((d, tn), lambda j, i: (0, j + nj)),
                  pl.BlockSpec((w_conv_out.shape[0], tn), col), pl.BlockSpec((w_attn_out.shape[0], tn), col)],
        out_specs=pl.BlockSpec((tm, tn), lambda j, i: (i, j)),
        out_shape=jax.ShapeDtypeStruct((m, d), BF16),
        compiler_params=_params(("parallel", "parallel")),
        name="merge_mixers",
    )(xb, ya_in, ob, w_merge, w_merge, w_conv_out, w_attn_out)


def _layer_norm(z, g, b):
    mu = jnp.mean(z, axis=-1, keepdims=True)
    zc = z - mu
    var = jnp.mean(zc * zc, axis=-1, keepdims=True)
    return zc * lax.rsqrt(var + LN_EPS) * g + b


def _out_proj_ln_kernel(m_ref, w_ref, x_ref, g_ref, b_ref, *rest, alpha):
    o_ref = rest[-1]
    z = alpha * x_ref[...] + _dot(m_ref[...], w_ref[...])
    o_ref[...] = _layer_norm(z, g_ref[...], b_ref[...])


def out_proj_ln(mixin, w_o, x, g, b, alpha, total_rows, row_offset, prev=None, tm=512):
    m, d = x.shape
    tm = min(tm, m)
    assert m % tm == 0 and row_offset % tm == 0
    off = row_offset // tm
    row = lambda i: (i, 0)
    fixed = lambda i: (0, 0)
    in_specs = [pl.BlockSpec((tm, d), row), pl.BlockSpec((d, d), fixed), pl.BlockSpec((tm, d), row),
                pl.BlockSpec((1, d), fixed), pl.BlockSpec((1, d), fixed)]
    args = [mixin, w_o, x, g, b]
    aliases = {}
    if prev is not None:
        in_specs.append(pl.BlockSpec(memory_space=pl.ANY))
        args.append(prev)
        aliases = {5: 0}
    return pl.pallas_call(
        functools.partial(_out_proj_ln_kernel, alpha=alpha),
        grid=(m // tm,),
        in_specs=in_specs,
        out_specs=pl.BlockSpec((tm, d), lambda i: (i + off, 0)),
        out_shape=jax.ShapeDtypeStruct((total_rows, d), F32),
        input_output_aliases=aliases,
        compiler_params=_params(("parallel",)),
        name="out_proj_ln",
    )(*args)


def _route_kernel(h_ref, wr_ref, rb_ref, u_ref, idx_ref, gate_ref, rank_ref, cnt_ref, run_ref, *, tm):
    i = pl.program_id(0)

    @pl.when(i == 0)
    def _():
        run_ref[...] = jnp.zeros(run_ref.shape, F32)

    logits = lax.dot_general(wr_ref[...], h_ref[...], (((1,), (1,)), ((), ())),
                             precision=lax.Precision.HIGHEST, preferred_element_type=F32)
    s = jax.nn.sigmoid(logits)
    bsc = s + rb_ref[...]
    gsz = N_EXPERTS // N_GROUPS
    x3 = bsc.reshape(N_GROUPS, gsz, tm)
    sub = lax.broadcasted_iota(I32, (N_GROUPS, gsz, tm), 1).astype(F32)
    m1 = jnp.max(x3, axis=1, keepdims=True)
    i1 = jnp.min(jnp.where(x3 == m1, sub, float(gsz)), axis=1, keepdims=True)
    m2 = jnp.max(jnp.where(sub == i1, -jnp.inf, x3), axis=1, keepdims=True)
    gsc = (m1 + m2).reshape(N_GROUPS, tm)
    giota = lax.broadcasted_iota(I32, (N_GROUPS, tm), 0)
    gsel = _top_k_mask(gsc, giota, 0, TOPK_GROUPS, N_GROUPS)
    emask = jnp.broadcast_to(gsel.reshape(N_GROUPS, 1, tm), (N_GROUPS, gsz, tm)).reshape(N_EXPERTS, tm) > 0.5
    val = jnp.where(emask, bsc, NEG_INF)
    eiota = lax.broadcasted_iota(I32, (N_EXPERTS, tm), 0).astype(F32)
    hits, idxs, sels = [], [], []
    for _ in range(MOE_TOPK):
        m = jnp.max(val, axis=0, keepdims=True)
        first = jnp.min(jnp.where(val == m, eiota, float(N_EXPERTS)), axis=0, keepdims=True)
        hit = eiota == first
        hits.append(hit)
        idxs.append(first.astype(I32))
        sels.append(jnp.sum(jnp.where(hit, s, 0.0), axis=0, keepdims=True))
        val = jnp.where(hit, -jnp.inf, val)
    den = sels[0]
    for k in range(1, MOE_TOPK):
        den = den + sels[k]
    onehot = hits[0].astype(F32)
    for k in range(1, MOE_TOPK):
        onehot = onehot + hits[k].astype(F32)
    before = _dot(onehot.astype(BF16), u_ref[...]) + run_ref[:, 0:1]
    pad = SUBLANES - MOE_TOPK
    zi = jnp.zeros((pad, tm), I32)
    zf = jnp.zeros((pad, tm), F32)
    ranks = [jnp.sum(jnp.where(hits[k], before, 0.0), axis=0, keepdims=True).astype(I32) for k in range(MOE_TOPK)]
    idx_ref[...] = jnp.concatenate(idxs + [zi], axis=0)
    gate_ref[...] = jnp.concatenate([sels[k] / den * ROUTED_SCALE for k in range(MOE_TOPK)] + [zf], axis=0)
    rank_ref[...] = jnp.concatenate(ranks + [zi], axis=0)
    run_ref[...] = run_ref[...] + jnp.sum(onehot, axis=1, keepdims=True)
    cnt_ref[...] = run_ref[...].astype(I32)


def route(h, w_router_t, router_bias, tm=ROUTE_TILE):
    t, d = h.shape
    assert t % tm == 0
    upper = (lax.broadcasted_iota(I32, (tm, tm), 0) < lax.broadcasted_iota(I32, (tm, tm), 1)).astype(BF16)
    tok = lambda i: (0, i)
    fixed = lambda i: (0, 0)
    return pl.pallas_call(
        functools.partial(_route_kernel, tm=tm),
        grid=(t // tm,),
        in_specs=[pl.BlockSpec((tm, d), lambda i: (i, 0)), pl.BlockSpec((N_EXPERTS, d), fixed),
                  pl.BlockSpec((N_EXPERTS, 1), fixed), pl.BlockSpec((tm, tm), fixed)],
        out_specs=[pl.BlockSpec((SUBLANES, tm), tok), pl.BlockSpec((SUBLANES, tm), tok),
                   pl.BlockSpec((SUBLANES, tm), tok), pl.BlockSpec((N_EXPERTS, LANES), fixed)],
        out_shape=[jax.ShapeDtypeStruct((SUBLANES, t), I32), jax.ShapeDtypeStruct((SUBLANES, t), F32),
                   jax.ShapeDtypeStruct((SUBLANES, t), I32), jax.ShapeDtypeStruct((N_EXPERTS, LANES), I32)],
        scratch_shapes=[pltpu.VMEM((N_EXPERTS, LANES), F32)],
        compiler_params=_params(("arbitrary",)),
        name="moe_route",
    )(h, w_router_t, router_bias.reshape(N_EXPERTS, 1), upper)


def _row_copy(src, src_row, dst, dst_row, sem):
    return pltpu.make_async_copy(src.at[pl.ds(src_row, 1)], dst.at[pl.ds(dst_row, 1)], sem)


def _dispatch_kernel(pe_ref, dest_ref, h_ref, xs_out, zbuf, sem, zsem, *, tb):

    @pl.when(pl.program_id(0) == 0)
    def _():
        zbuf[...] = jnp.zeros(zbuf.shape, zbuf.dtype)

        def nonempty(e):
            return pe_ref[e] > jnp.where(e > 0, pe_ref[jnp.maximum(e - 1, 0)], 0)

        def zero_tile(e):
            first = pl.multiple_of(pe_ref[e] - EXPERT_TILE, EXPERT_TILE)
            return pltpu.make_async_copy(zbuf, xs_out.at[pl.ds(first, EXPERT_TILE)], zsem)

        def zstart(e, carry):
            @pl.when(nonempty(e))
            def _():
                zero_tile(e).start()
            return carry

        def zwait(e, carry):
            @pl.when(nonempty(e))
            def _():
                zero_tile(e).wait()
            return carry

        lax.fori_loop(0, N_EXPERTS, zstart, 0)
        lax.fori_loop(0, N_EXPERTS, zwait, 0)

    def start(r, carry):
        for k in range(MOE_TOPK):
            _row_copy(h_ref, r, xs_out, dest_ref[0, k, r], sem).start(priority=k % DMA_PRIORITIES)
        return carry

    lax.fori_loop(0, tb, start, 0, unroll=DMA_UNROLL)
    for k in range(MOE_TOPK):
        pltpu.make_async_copy(h_ref, xs_out.at[pl.ds(0, tb)], sem).wait()


def dispatch(h, dest_tiles, pad_end, n_rows, tb=SCATTER_TILE):
    t, d = h.shape
    assert t % tb == 0
    grid_spec = pltpu.PrefetchScalarGridSpec(
        num_scalar_prefetch=1,
        grid=(t // tb,),
        in_specs=[pl.BlockSpec((1, SUBLANES, tb), lambda i, pe: (i, 0, 0), memory_space=pltpu.SMEM),
                  pl.BlockSpec((tb, d), lambda i, pe: (i, 0))],
        out_specs=pl.BlockSpec(memory_space=pl.ANY),
        scratch_shapes=[pltpu.VMEM((EXPERT_TILE, d), F32), pltpu.SemaphoreType.DMA, pltpu.SemaphoreType.DMA],
    )
    return pl.pallas_call(
        functools.partial(_dispatch_kernel, tb=tb),
        grid_spec=grid_spec,
        out_shape=jax.ShapeDtypeStruct((n_rows, d), F32),
        compiler_params=pltpu.CompilerParams(dimension_semantics=("arbitrary",), has_side_effects=True),
        name="moe_dispatch",
    )(pad_end, dest_tiles, h)


def _col_chunks(n, width=2 * LANES):
    return [(c, min(c + width, n)) for c in range(0, n, width)]


def _expert_up_kernel(te_ref, nu_ref, x_ref, w1_ref, w3_ref, h_ref):
    i = pl.program_id(0)

    @pl.when(i < nu_ref[0])
    def _():
        x = x_ref[...].astype(BF16)
        for c0, c1 in _col_chunks(h_ref.shape[1]):
            a = _dot(x, w1_ref[0, :, c0:c1].astype(BF16))
            b = _dot(x, w3_ref[0, :, c0:c1].astype(BF16))
            h_ref[:, c0:c1] = ((a * jax.nn.sigmoid(a)) * b).astype(h_ref.dtype)

    @pl.when(i >= nu_ref[0])
    def _():
        h_ref[...] = jnp.zeros(h_ref.shape, h_ref.dtype)


def _expert_down_kernel(te_ref, nu_ref, h_ref, w2_ref, o_ref):
    i = pl.program_id(0)

    @pl.when(i < nu_ref[0])
    def _():
        h = h_ref[...]
        for c0, c1 in _col_chunks(o_ref.shape[1]):
            o_ref[:, c0:c1] = _dot(h, w2_ref[0, :, c0:c1].astype(BF16))

    @pl.when(i >= nu_ref[0])
    def _():
        o_ref[...] = jnp.zeros(o_ref.shape, o_ref.dtype)


def expert_mlp_f32(xs, tile_expert, n_used, w1, w3, w2, tm=EXPERT_TILE):
    r, d = xs.shape
    f = w1.shape[2]
    assert r % tm == 0
    nt = r // tm
    rows = lambda i, te, nu: (jnp.minimum(i, nu[0] - 1), 0)
    own = lambda i, te, nu: (i, 0)
    wsel = lambda i, te, nu: (te[jnp.minimum(i, nu[0] - 1)], 0, 0)
    big = pltpu.CompilerParams(dimension_semantics=("arbitrary",), vmem_limit_bytes=60 * 1024 * 1024)
    h = pl.pallas_call(
        _expert_up_kernel,
        grid_spec=pltpu.PrefetchScalarGridSpec(
            num_scalar_prefetch=2, grid=(nt,),
            in_specs=[pl.BlockSpec((tm, d), rows), pl.BlockSpec((1, d, f), wsel), pl.BlockSpec((1, d, f), wsel)],
            out_specs=pl.BlockSpec((tm, f), own)),
        out_shape=jax.ShapeDtypeStruct((r, f), BF16),
        compiler_params=big,
        name="expert_up",
    )(tile_expert, n_used, xs, w1, w3)
    return pl.pallas_call(
        _expert_down_kernel,
        grid_spec=pltpu.PrefetchScalarGridSpec(
            num_scalar_prefetch=2, grid=(nt,),
            in_specs=[pl.BlockSpec((tm, f), rows), pl.BlockSpec((1, f, d), wsel)],
            out_specs=pl.BlockSpec((tm, d), own)),
        out_shape=jax.ShapeDtypeStruct((r, d), F32),
        compiler_params=big,
        name="expert_down",
    )(tile_expert, n_used, h, w2)


def _expert_kernel(te_ref, nu_ref, x_ref, w1_ref, w3_ref, w2_ref, o_ref):
    i = pl.program_id(0)

    @pl.when(i < nu_ref[0])
    def _():
        x = x_ref[...].astype(BF16)
        a = _dot(x, w1_ref[0])
        h = (a * jax.nn.sigmoid(a)) * _dot(x, w3_ref[0])
        o_ref[...] = _dot(h.astype(BF16), w2_ref[0])

    @pl.when(i >= nu_ref[0])
    def _():
        o_ref[...] = jnp.zeros(o_ref.shape, o_ref.dtype)


def expert_mlp(xs, tile_expert, n_used, w1, w3, w2, tm=EXPERT_TILE):
    r, d = xs.shape
    f = w1.shape[2]
    assert r % tm == 0
    nt = r // tm
    rows = lambda i, te, nu: (jnp.minimum(i, nu[0] - 1), 0)
    wsel = lambda i, te, nu: (te[jnp.minimum(i, nu[0] - 1)], 0, 0)
    grid_spec = pltpu.PrefetchScalarGridSpec(
        num_scalar_prefetch=2,
        grid=(nt,),
        in_specs=[pl.BlockSpec((tm, d), rows), pl.BlockSpec((1, d, f), wsel),
                  pl.BlockSpec((1, d, f), wsel), pl.BlockSpec((1, f, d), wsel)],
        out_specs=pl.BlockSpec((tm, d), lambda i, te, nu: (i, 0)),
    )
    return pl.pallas_call(
        _expert_kernel,
        grid_spec=grid_spec,
        out_shape=jax.ShapeDtypeStruct((r, d), F32),
        compiler_params=_params(("arbitrary",)),
        name="expert_mlp",
    )(tile_expert, n_used, xs, w1, w3, w2)


def _combine_kernel(dest_ref, next_ref, gate_ref, h_ref, sh_ref, g_ref, b_ref, o_hbm, y_ref, buf, sems,
                    *, tb, alpha, n_steps):
    i = pl.program_id(0)

    def fetch(d_ref, s):
        def start(r, carry):
            for k in range(MOE_TOPK):
                _row_copy(o_hbm, d_ref[0, k, r], buf.at[s, k], r, sems.at[s]).start(priority=k % DMA_PRIORITIES)
            return carry

        lax.fori_loop(0, tb, start, 0, unroll=DMA_UNROLL)

    def finish(s):
        for k in range(MOE_TOPK):
            pltpu.make_async_copy(o_hbm.at[pl.ds(0, tb)], buf.at[s, k], sems.at[s]).wait()
        gate = gate_ref[...]
        routed = gate[:, 0:1] * buf[s, 0]
        for k in range(1, MOE_TOPK):
            routed = routed + gate[:, k:k + 1] * buf[s, k]
        z = alpha * h_ref[...] + (routed + sh_ref[...])
        y_ref[...] = _layer_norm(z, g_ref[...], b_ref[...])

    @pl.when(i == 0)
    def _():
        fetch(dest_ref, 0)

    for s in range(2):
        @pl.when((i + 1 < n_steps) & ((i + 1) % 2 == s))
        def _(s=s):
            fetch(next_ref, s)

    for s in range(2):
        @pl.when(i % 2 == s)
        def _(s=s):
            finish(s)


def combine_ln(dest_tiles, gate_t, h, shared, o_sorted, g, b, alpha, tb=COMBINE_TILE):
    t, d = h.shape
    assert t % tb == 0
    n_steps = t // tb
    row = lambda i: (i, 0)
    fixed = lambda i: (0, 0)
    return pl.pallas_call(
        functools.partial(_combine_kernel, tb=tb, alpha=alpha, n_steps=n_steps),
        grid=(n_steps,),
        in_specs=[pl.BlockSpec((1, SUBLANES, tb), lambda i: (i, 0, 0), memory_space=pltpu.SMEM),
                  pl.BlockSpec((1, SUBLANES, tb), lambda i: (jnp.minimum(i + 1, n_steps - 1), 0, 0),
                               memory_space=pltpu.SMEM),
                  pl.BlockSpec((tb, SUBLANES), row), pl.BlockSpec((tb, d), row), pl.BlockSpec((tb, d), row),
                  pl.BlockSpec((1, d), fixed), pl.BlockSpec((1, d), fixed),
                  pl.BlockSpec(memory_space=pl.ANY)],
        out_specs=pl.BlockSpec((tb, d), row),
        out_shape=jax.ShapeDtypeStruct((t, d), F32),
        scratch_shapes=[pltpu.VMEM((2, MOE_TOPK, tb, d), F32), pltpu.SemaphoreType.DMA((2,))],
        compiler_params=_params(("arbitrary",)),
        name="moe_combine",
    )(dest_tiles, dest_tiles, gate_t, h, shared, g, b, o_sorted)


def moe_ffn_ln(h1, w_router, router_bias, w1, w3, w2, ws1, ws3, ws2, g, b, alpha):
    t, d = h1.shape
    idx, gate, rank, cnt = route(h1, w_router.T, router_bias)
    counts = cnt[:, 0]
    padded = (counts + EXPERT_TILE - 1) // EXPERT_TILE * EXPERT_TILE
    e_ids = jnp.arange(N_EXPERTS, dtype=I32)
    pad_end = jnp.sum(jnp.where(e_ids[None, :] <= e_ids[:, None], padded[None, :], 0), axis=1)
    pad_start = pad_end - padded
    n_tiles = -(-(t * MOE_TOPK) // EXPERT_TILE) + N_EXPERTS
    tile_start = jnp.arange(n_tiles, dtype=I32) * EXPERT_TILE
    tile_expert = jnp.minimum(jnp.sum((pad_end[None, :] <= tile_start[:, None]).astype(I32), axis=1),
                              N_EXPERTS - 1)
    n_used = (pad_end[-1] // EXPERT_TILE).astype(I32).reshape(1)
    dest = jnp.sum(jnp.where(idx[:, :, None] == e_ids, pad_start, 0), axis=-1) + rank

    def tiles(a, tb):
        return a.reshape(SUBLANES, t // tb, tb).transpose(1, 0, 2)

    xs = dispatch(h1, tiles(dest, SCATTER_TILE), pad_end.astype(I32), n_tiles * EXPERT_TILE)
    o_sorted = expert_mlp_f32(xs, tile_expert, n_used, w1, w3, w2)
    n_dense = t // ROUTE_TILE
    shared = expert_mlp_f32(h1, jnp.zeros((n_dense,), I32), jnp.full((1,), n_dense, I32),
                            ws1[None], ws3[None], ws2[None], tm=ROUTE_TILE)
    return combine_ln(tiles(dest, COMBINE_TILE), gate.T, h1, shared, o_sorted, g, b, alpha)


def _split_w_in(w_in, d_conv):
    d = w_in.shape[0]
    hd = N_HEADS * HEAD_DIM
    kvw = 2 * N_KV * HEAD_DIM
    o = 0
    parts = {}
    for name, width in (("b", d_conv), ("c", d_conv), ("h", d_conv), ("q", hd), ("cmp", kvw), ("sel", kvw),
                        ("win", kvw), ("nsa", 3 * N_HEADS), ("merge", 2 * d)):
        parts[name] = w_in[:, o:o + width]
        o += width
    nsa = parts["nsa"].reshape(d, N_KV, HPG * 3)
    parts["nsa"] = jnp.pad(nsa, ((0, 0), (0, 0), (0, LANES - HPG * 3))).reshape(d, N_KV * LANES)
    return {k: v.astype(BF16) for k, v in parts.items()}


def kernel(x_prompt, x_sample, cache_cmp_kv, cache_sel_kv, state_win_kv, state_conv, page_table,
           w_in, conv_w, w_phi_k, w_phi_v, w_conv_out, w_attn_out, w_o, ln1_g, ln1_b,
           w_router, router_bias, w_e_gate, w_e_up, w_e_down, w_s_gate, w_s_up, w_s_down,
           ln2_g, ln2_b):
    depth = w_in.shape[0]
    assert depth == 1
    alpha = (2.0 * depth) ** 0.25
    bp, tp, d = x_prompt.shape
    ns_, ts, _ = x_sample.shape
    assert ts == 1
    n_pages = page_table.shape[1]
    past_len = n_pages * PAGE_SIZE
    d_conv = conv_w.shape[2]
    kvw = 2 * N_KV * HEAD_DIM
    gw = N_KV * HEAD_DIM
    hd = N_HEADS * HEAD_DIM
    t_all = bp * tp + ns_

    w = _split_w_in(w_in[0], d_conv)
    cw = conv_w[0]
    wk_phi = w_phi_k[0].reshape(CMP_BLOCK * HEAD_DIM, HEAD_DIM).astype(BF16)
    wv_phi = w_phi_v[0].reshape(CMP_BLOCK * HEAD_DIM, HEAD_DIM).astype(BF16)
    wco = w_conv_out[0].astype(BF16)
    wao = w_attn_out[0].astype(BF16)
    wo = w_o[0].astype(BF16)
    g1, b1 = ln1_g[0].reshape(1, d), ln1_b[0].reshape(1, d)
    g2, b2 = ln2_g[0].reshape(1, d), ln2_b[0].reshape(1, d)

    xpb = x_prompt.astype(BF16)
    xp2 = xpb.reshape(bp * tp, d)
    ya_p, st_p = conv_proj(xpb, w["b"], w["c"], w["h"], cw, jnp.zeros((bp, SUBLANES, d_conv), F32))
    (q_p,) = matmul(xp2, w["q"], (BF16,))
    (cmp_p,) = kv_proj(xp2, w["cmp"], False)
    sel_p, sel_pb = kv_proj(xp2, w["sel"], True)
    win_p, win_pb = kv_proj(xp2, w["win"], True)
    (ng_p,) = matmul(xp2, w["nsa"], (F32,))
    nc_p = tp // CMP_BLOCK
    slabs = lambda a: a.reshape(-1, HEAD_DIM)
    kc_p, vc_p = cmp_kv(cmp_p, wk_phi, wv_phi, nb=nc_p)
    pad_c = lambda a: jnp.pad(a, ((0, 0), (0, 0), (0, LANES - a.shape[2]), (0, 0))).astype(BF16)
    blk_of_key = jnp.arange(tp, dtype=I32) // SEL_BLOCK
    expand = (jnp.arange(LANES, dtype=I32)[:, None] == 2 * blk_of_key[None, :]).astype(BF16)
    o_p = prompt_attention(q_p.reshape(bp, tp, hd), ng_p.reshape(bp, tp, N_KV * LANES), pad_c(kc_p), pad_c(vc_p),
                           sel_pb.reshape(bp, tp, kvw), win_pb.reshape(bp, tp, kvw), expand)
    mix_p = merge_mixers(xp2, ya_p.reshape(bp * tp, d_conv), o_p.reshape(bp * tp, hd), w["merge"], wco, wao)
    h1 = out_proj_ln(mix_p, wo, x_prompt.reshape(bp * tp, d), g1, b1, alpha, t_all, 0,
                     prev=jnp.zeros((t_all, d), F32))

    xs2 = x_sample.reshape(ns_, d)
    xsb = xs2.astype(BF16)
    w_conv3 = jnp.concatenate([w["b"], w["c"], w["h"]], axis=1)
    (pconv_s,) = matmul(xsb, w_conv3, (F32,))
    ya_s, u_s = step_conv(pconv_s, state_conv[0, :, 0], state_conv[0, :, 1], cw)
    (q_s,) = matmul(xsb, w["q"], (BF16,))
    (cmp_s,) = matmul(xsb, w["cmp"], (F32,))
    (sel_s,) = matmul(xsb, w["sel"], (F32,))
    (win_s,) = matmul(xsb, w["win"], (F32,))
    (ng_s,) = matmul(xsb, w["nsa"], (F32,))
    n_phys = cache_cmp_kv.shape[1]
    bpp = PAGE_SIZE // CMP_BLOCK
    pages_per_tile = 32
    assert n_phys % pages_per_tile == 0
    kc_all, vc_all = cmp_kv(slabs(cache_cmp_kv), wk_phi, wv_phi, nb=pages_per_tile * bpp)

    def per_sequence(a):
        a = a.reshape(n_phys // pages_per_tile, N_KV, pages_per_tile, bpp, HEAD_DIM)
        a = a.transpose(0, 2, 3, 1, 4).reshape(n_phys, bpp, gw)
        a = a[page_table].reshape(ns_, n_pages * bpp, gw)
        return jnp.pad(a, ((0, 0), (0, LANES - n_pages * bpp), (0, 0))).astype(BF16)

    head_group = jnp.arange(LANES, dtype=I32) // HPG
    qh = jnp.pad(q_s.reshape(ns_, N_HEADS, HEAD_DIM), ((0, 0), (0, LANES - N_HEADS), (0, 0)))
    qblk = jnp.where(jnp.arange(N_KV, dtype=I32)[None, :, None, None] == head_group[None, None, None, :],
                     qh.transpose(0, 2, 1)[:, None], 0).reshape(ns_, gw, LANES).astype(BF16)
    nk = n_pages * PAGE_SIZE
    key_row = jnp.arange(nk + 2 * SUBLANES, dtype=I32)
    expand_t = ((2 * (key_row // SEL_BLOCK))[:, None] == jnp.arange(LANES, dtype=I32)[None, :]) & (key_row <= nk)[:, None]
    o_s, win_next = step_attention(qblk, ng_s.reshape(ns_, 1, N_KV * LANES), per_sequence(kc_all), per_sequence(vc_all),
                         slabs(cache_sel_kv), page_table, sel_s.reshape(ns_, 1, kvw), slabs(state_win_kv),
                         win_s.reshape(ns_, 1, kvw), expand_t.astype(BF16), past_len)
    mix_s = merge_mixers(xsb, ya_s, o_s.reshape(ns_, hd), w["merge"], wco, wao)
    h1 = out_proj_ln(mix_s, wo, xs2, g1, b1, alpha, t_all, bp * tp, prev=h1)

    y = moe_ffn_ln(h1, w_router[0], router_bias[0], w_e_gate[0], w_e_up[0], w_e_down[0],
                   w_s_gate[0], w_s_up[0], w_s_down[0], g2, b2, alpha)

    kv6 = lambda a, n, t: a.reshape(1, n, t, 2, N_KV, HEAD_DIM)
    win_keep = min(WINDOW, tp)
    new_win_s = win_next.reshape(state_win_kv.shape)
    new_conv_s = jnp.stack([state_conv[0, :, 1], u_s], axis=1)[None]
    return (y[:bp * tp].reshape(bp, tp, d), y[bp * tp:].reshape(ns_, ts, d),
            kv6(cmp_p, bp, tp), kv6(sel_p, bp, tp), kv6(win_p, bp, tp)[:, :, tp - win_keep:],
            st_p[None, :, SUBLANES - (CONV_WIDTH - 1):],
            kv6(cmp_s, ns_, 1), kv6(sel_s, ns_, 1), new_win_s, new_conv_s)
```

```python
import functools

import jax
import jax.numpy as jnp
from jax import lax
from jax.experimental import pallas as pl
from jax.experimental.pallas import tpu as pltpu

F32 = jnp.float32
BF16 = jnp.bfloat16
I32 = jnp.int32

N_HEADS = 16
HEAD_DIM = 128
N_KV = 4
HPG = N_HEADS // N_KV
CMP_BLOCK = 32
SEL_BLOCK = 64
SEL_TOPK = 8
WINDOW = 512
PAGE_SIZE = 128
CONV_WIDTH = 3
N_EXPERTS = 64
MOE_TOPK = 6
N_GROUPS = 8
TOPK_GROUPS = 4
ROUTED_SCALE = 2.5
LN_EPS = 1e-5
NEG_INF = -1e30
FORCE_SCORE = 1e4
ATTN_SCALE = HEAD_DIM ** -0.5
LOG2_E = 1.4426950408889634

LANES = 128
SUBLANES = 8
VMEM_LIMIT = 56 * 1024 * 1024

EXPERT_TILE = 256
ROUTE_TILE = 384
COMBINE_TILE = 128
SCATTER_TILE = 128
DMA_UNROLL = 8
DMA_PRIORITIES = 2


def _dot(a, b):
    return jnp.dot(a, b, preferred_element_type=F32)


def _dot_nt(a, b):
    return lax.dot_general(a, b, (((1,), (1,)), ((), ())), preferred_element_type=F32)


def _dot_tn(a, b):
    return lax.dot_general(a, b, (((0,), (0,)), ((), ())), preferred_element_type=F32)


def _params(sem):
    return pltpu.CompilerParams(dimension_semantics=sem, vmem_limit_bytes=VMEM_LIMIT)


def _mm_kernel(x_ref, w_ref, *o_refs):
    r = _dot(x_ref[...], w_ref[...])
    for o_ref in o_refs:
        o_ref[...] = r.astype(o_ref.dtype)


def matmul(x, w, out_dtypes, tm=1024, tn=1024):
    m, k = x.shape
    n = w.shape[1]
    tm = min(tm, m)
    tn = min(tn, n)
    assert m % tm == 0 and n % tn == 0
    outs = pl.pallas_call(
        _mm_kernel,
        grid=(n // tn, m // tm),
        in_specs=[pl.BlockSpec((tm, k), lambda j, i: (i, 0)),
                  pl.BlockSpec((k, tn), lambda j, i: (0, j))],
        out_specs=[pl.BlockSpec((tm, tn), lambda j, i: (i, j)) for _ in out_dtypes],
        out_shape=[jax.ShapeDtypeStruct((m, n), dt) for dt in out_dtypes],
        compiler_params=_params(("parallel", "parallel")),
        name="proj_mm",
    )(x, w)
    return outs


def _kv_proj_kernel(x_ref, w_ref, slab_ref, *b_refs):
    r = _dot(x_ref[...], w_ref[...])
    tm, n = r.shape
    nslab = n // HEAD_DIM
    for c in range(nslab):
        slab_ref[pl.ds(c, tm, stride=nslab), :] = r[:, c * HEAD_DIM:(c + 1) * HEAD_DIM]
    for b_ref in b_refs:
        b_ref[...] = r.astype(b_ref.dtype)


def kv_proj(x, w, with_bf16, tm=1024):
    m, k = x.shape
    n = w.shape[1]
    nslab = n // HEAD_DIM
    tm = min(tm, m)
    assert m % tm == 0
    out_specs = [pl.BlockSpec((tm * nslab, HEAD_DIM), lambda i: (i, 0))]
    out_shape = [jax.ShapeDtypeStruct((m * nslab, HEAD_DIM), F32)]
    if with_bf16:
        out_specs.append(pl.BlockSpec((tm, n), lambda i: (i, 0)))
        out_shape.append(jax.ShapeDtypeStruct((m, n), BF16))
    return pl.pallas_call(
        _kv_proj_kernel,
        grid=(m // tm,),
        in_specs=[pl.BlockSpec((tm, k), lambda i: (i, 0)), pl.BlockSpec((k, n), lambda i: (0, 0))],
        out_specs=out_specs,
        out_shape=out_shape,
        compiler_params=_params(("parallel",)),
        name="kv_proj",
    )(x, w)


def _conv_proj_kernel(x_ref, wb_ref, wc_ref, wh_ref, cw_ref, pre_ref, ya_ref, st_ref, s_ref, *, tm):
    i = pl.program_id(2)

    @pl.when(i == 0)
    def _():
        s_ref[0:SUBLANES, :] = pre_ref[0]

    x = x_ref[0]
    pb = _dot(x, wb_ref[...])
    u = _dot(x, wc_ref[...]) * _dot(x, wh_ref[...])
    s_ref[SUBLANES:SUBLANES + tm, :] = u
    u1 = s_ref[SUBLANES - 1:SUBLANES - 1 + tm, :]
    u2 = s_ref[SUBLANES - 2:SUBLANES - 2 + tm, :]
    cw = cw_ref[...]
    y = cw[0:1] * u2 + cw[1:2] * u1 + cw[2:3] * u
    ya_ref[0] = (pb * y).astype(ya_ref.dtype)
    last = s_ref[tm:tm + SUBLANES, :]
    s_ref[0:SUBLANES, :] = last
    st_ref[0] = last


def conv_proj(xb, wb, wc, wh, conv_w, prefix8, tm=512, tn=1024):
    b, t, d = xb.shape
    dc = wb.shape[1]
    tm = min(tm, t)
    assert t % tm == 0 and dc % tn == 0 and tm >= SUBLANES
    return pl.pallas_call(
        functools.partial(_conv_proj_kernel, tm=tm),
        grid=(dc // tn, b, t // tm),
        in_specs=[pl.BlockSpec((1, tm, d), lambda j, bb, i: (bb, i, 0)),
                  pl.BlockSpec((d, tn), lambda j, bb, i: (0, j)),
                  pl.BlockSpec((d, tn), lambda j, bb, i: (0, j)),
                  pl.BlockSpec((d, tn), lambda j, bb, i: (0, j)),
                  pl.BlockSpec((CONV_WIDTH, tn), lambda j, bb, i: (0, j)),
                  pl.BlockSpec((1, SUBLANES, tn), lambda j, bb, i: (bb, 0, j))],
        out_specs=[pl.BlockSpec((1, tm, tn), lambda j, bb, i: (bb, i, j)),
                   pl.BlockSpec((1, SUBLANES, tn), lambda j, bb, i: (bb, 0, j))],
        out_shape=[jax.ShapeDtypeStruct((b, t, dc), BF16),
                   jax.ShapeDtypeStruct((b, SUBLANES, dc), F32)],
        scratch_shapes=[pltpu.VMEM((tm + SUBLANES, tn), F32)],
        compiler_params=_params(("parallel", "parallel", "arbitrary")),
        name="conv_proj",
    )(xb, wb, wc, wh, conv_w, prefix8)


def _step_conv_kernel(p_ref, s0_ref, s1_ref, cw_ref, ya_ref, u_ref, *, dc):
    pb = p_ref[:, 0:dc]
    u = p_ref[:, dc:2 * dc] * p_ref[:, 2 * dc:3 * dc]
    cw = cw_ref[...]
    y = cw[0:1] * s0_ref[...] + cw[1:2] * s1_ref[...] + cw[2:3] * u
    ya_ref[...] = (pb * y).astype(ya_ref.dtype)
    u_ref[...] = u


def step_conv(p, s0, s1, conv_w):
    n, dc3 = p.shape
    dc = dc3 // 3
    return pl.pallas_call(
        functools.partial(_step_conv_kernel, dc=dc),
        out_shape=[jax.ShapeDtypeStruct((n, dc), BF16), jax.ShapeDtypeStruct((n, dc), F32)],
        compiler_params=pltpu.CompilerParams(vmem_limit_bytes=VMEM_LIMIT),
        name="step_conv",
    )(p, s0, s1, conv_w)


def _cmp_kv_kernel(x_ref, wk_ref, wv_ref, ok_ref, ov_ref, *, nb):
    nslab = 2 * N_KV

    def rows(l, c):
        return x_ref[pl.ds(l * nslab + c, nb, stride=CMP_BLOCK * nslab), :].astype(BF16)

    for kv, (w_ref, o_ref) in enumerate(((wk_ref, ok_ref), (wv_ref, ov_ref))):
        per_group = [jnp.concatenate([rows(l, kv * N_KV + g) for l in range(CMP_BLOCK)], axis=1)
                     for g in range(N_KV)]
        lhs = jnp.concatenate(per_group, axis=0)
        r = _dot(lhs, w_ref[...])
        for g in range(N_KV):
            o_ref[0, g] = r[g * nb:(g + 1) * nb]


def cmp_kv(raw, wk, wv, nb):
    n_rows, width = raw.shape
    rows_per_step = nb * CMP_BLOCK * 2 * N_KV
    assert n_rows % rows_per_step == 0 and width == HEAD_DIM
    nt = n_rows // rows_per_step
    return pl.pallas_call(
        functools.partial(_cmp_kv_kernel, nb=nb),
        grid=(nt,),
        in_specs=[pl.BlockSpec((rows_per_step, width), lambda i: (i, 0)),
                  pl.BlockSpec(wk.shape, lambda i: (0, 0)),
                  pl.BlockSpec(wv.shape, lambda i: (0, 0))],
        out_specs=[pl.BlockSpec((1, N_KV, nb, HEAD_DIM), lambda i: (i, 0, 0, 0))] * 2,
        out_shape=[jax.ShapeDtypeStruct((nt, N_KV, nb, HEAD_DIM), F32)] * 2,
        compiler_params=_params(("parallel",)),
        name="cmp_kv",
    )(raw, wk, wv)


def _top_k_mask(val, iota, axis, k, size):
    sel = jnp.zeros(val.shape, F32)
    iota = iota.astype(F32)
    for _ in range(k):
        m = jnp.max(val, axis=axis, keepdims=True)
        first = jnp.min(jnp.where(val == m, iota, float(size)), axis=axis, keepdims=True)
        hit = iota == first
        sel = jnp.where(hit, 1.0, sel)
        val = jnp.where(hit, -jnp.inf, val)
    return sel


def _prompt_attn_kernel(q_ref, ng_ref, kc_ref, vc_ref, ks_ref, vs_ref, kw_ref, vw_ref, e_ref, o_ref,
                        m_ref, l_ref, acc_ref, oacc_ref, *, tq, tks, tkw, nc, ns):
    qt = pl.program_id(2)
    t0 = qt * tq
    qpos = t0 + lax.broadcasted_iota(I32, (tq, 1), 0)
    gates = jax.nn.sigmoid(ng_ref[0])

    qcol = t0 + lax.broadcasted_iota(I32, (1, tq), 1)
    rowb = lax.broadcasted_iota(I32, (LANES, tq), 0)
    vis = ((rowb + 1) * CMP_BLOCK - 1 <= qcol) & (rowb < nc)
    visf = vis.astype(F32)
    kc = kc_ref[0, 0]
    vc = vc_ref[0, 0]
    imp = jnp.zeros((LANES, tq), F32)
    for h in range(HPG):
        qh = q_ref[0, :, h * HEAD_DIM:(h + 1) * HEAD_DIM]
        s = jnp.where(vis, _dot_nt(kc, qh) * ATTN_SCALE, NEG_INF)
        e = jnp.exp(s - jnp.max(s, axis=0, keepdims=True))
        p = e / jnp.sum(e, axis=0, keepdims=True) * visf
        imp = imp + p
        oacc_ref[h] = gates[:, h * 3:h * 3 + 1] * _dot_tn(p.astype(BF16), vc)

    pair = imp + pltpu.roll(imp, LANES - 1, axis=0)
    blk = rowb >> 1
    cand = ((rowb & 1) == 0) & (blk < ns)
    cur = qcol >> (SEL_BLOCK.bit_length() - 1)
    forced = (blk == 0) | (blk == cur) | (blk == cur - 1)
    val = jnp.where(blk * SEL_BLOCK <= qcol, jnp.where(forced, FORCE_SCORE, pair), -1.0)
    val = jnp.where(cand, val, -jnp.inf)
    selb = _top_k_mask(val, rowb, 0, SEL_TOPK, LANES).astype(BF16)

    def flash(k_ref, v_ref, tk, kt_lo, kt_hi, mask_fn, branch):
        m_ref[...] = jnp.full(m_ref.shape, NEG_INF, F32)
        l_ref[...] = jnp.zeros(l_ref.shape, F32)
        acc_ref[...] = jnp.zeros(acc_ref.shape, F32)

        def body(kt, carry):
            k0 = pl.multiple_of(kt * tk, tk)
            k = k_ref[0, pl.ds(k0, tk), :]
            v = v_ref[0, pl.ds(k0, tk), :]
            mask = mask_fn(k0, tk)
            for h in range(HPG):
                qh = q_ref[0, :, h * HEAD_DIM:(h + 1) * HEAD_DIM]
                s = jnp.where(mask, _dot_nt(qh, k) * (ATTN_SCALE * LOG2_E), NEG_INF)
                m_prev = m_ref[h]
                m_next = jnp.maximum(m_prev, jnp.max(s, axis=-1, keepdims=True))
                alpha = jnp.exp2(m_prev - m_next)
                p = jnp.exp2(s - jnp.tile(m_next, (1, tk // LANES)))
                psum = p[:, 0:LANES]
                for c in range(1, tk // LANES):
                    psum = psum + p[:, c * LANES:(c + 1) * LANES]
                l_ref[h] = alpha * l_ref[h] + psum
                acc_ref[h] = alpha * acc_ref[h] + _dot(p.astype(BF16), v)
                m_ref[h] = m_next
            return carry

        lax.fori_loop(kt_lo, kt_hi, body, 0)
        for h in range(HPG):
            c = h * 3 + branch
            l_tot = jnp.sum(l_ref[h], axis=-1, keepdims=True)
            oacc_ref[h] = oacc_ref[h] + gates[:, c:c + 1] * (acc_ref[h] / l_tot)

    def sel_mask(k0, tk):
        kpos = k0 + lax.broadcasted_iota(I32, (1, tk), 1)
        chosen = _dot_tn(selb, e_ref[:, pl.ds(k0, tk)]) > 0.5
        return chosen & (kpos <= qpos)

    def win_mask(k0, tk):
        kpos = k0 + lax.broadcasted_iota(I32, (1, tk), 1)
        return (kpos <= qpos) & (kpos >= qpos - WINDOW)

    last = t0 + tq - 1
    flash(ks_ref, vs_ref, tks, 0, last // tks + 1, sel_mask, 1)
    flash(kw_ref, vw_ref, tkw, jnp.maximum(t0 - WINDOW, 0) // tkw, last // tkw + 1, win_mask, 2)
    for h in range(HPG):
        o_ref[0, :, h * HEAD_DIM:(h + 1) * HEAD_DIM] = oacc_ref[h].astype(o_ref.dtype)


def prompt_attention(q, ng, kc, vc, sel_b, win_b, expand, tq=512, tks=256, tkw=256):
    b, t, _ = q.shape
    tks = min(tks, t)
    assert t % tq == 0 and t % tks == 0 and t % tkw == 0
    nc = t // CMP_BLOCK
    ns = max(-(-t // SEL_BLOCK), SEL_TOPK)
    assert nc <= LANES and 2 * ns <= LANES and ns * SEL_BLOCK == t
    gw = HPG * HEAD_DIM
    kern = functools.partial(_prompt_attn_kernel, tq=tq, tks=tks, tkw=tkw, nc=nc, ns=ns)
    kv_spec = lambda off: pl.BlockSpec((1, t, HEAD_DIM), lambda bb, g, i: (bb, 0, off + g))
    return pl.pallas_call(
        kern,
        grid=(b, N_KV, t // tq),
        in_specs=[pl.BlockSpec((1, tq, gw), lambda bb, g, i: (bb, i, g)),
                  pl.BlockSpec((1, tq, LANES), lambda bb, g, i: (bb, i, g)),
                  pl.BlockSpec((1, 1, LANES, HEAD_DIM), lambda bb, g, i: (bb, g, 0, 0)),
                  pl.BlockSpec((1, 1, LANES, HEAD_DIM), lambda bb, g, i: (bb, g, 0, 0)),
                  kv_spec(0), kv_spec(N_KV), kv_spec(0), kv_spec(N_KV),
                  pl.BlockSpec((LANES, t), lambda bb, g, i: (0, 0))],
        out_specs=pl.BlockSpec((1, tq, gw), lambda bb, g, i: (bb, i, g)),
        out_shape=jax.ShapeDtypeStruct(q.shape, BF16),
        scratch_shapes=[pltpu.VMEM((HPG, tq, LANES), F32), pltpu.VMEM((HPG, tq, LANES), F32),
                        pltpu.VMEM((HPG, tq, HEAD_DIM), F32), pltpu.VMEM((HPG, tq, HEAD_DIM), F32)],
        compiler_params=_params(("parallel", "parallel", "arbitrary")),
        name="prompt_attn",
    )(q, ng, kc, vc, sel_b, sel_b, win_b, win_b, expand)


def _step_attn_kernel(*refs, n_pages, past_len, nc, ns):
    (_, qb_ref, ng_ref, kc_ref, vc_ref) = refs[:5]
    pages = refs[5:5 + n_pages]
    (snew_ref, win_ref, wnew_ref, wslab_ref, et_ref, o_ref, wout_ref, kbuf, vbuf, kwbuf, vwbuf) = refs[5 + n_pages:]
    gw = N_KV * HEAD_DIM
    qb = qb_ref[0]
    lane = lax.broadcasted_iota(I32, (LANES, LANES), 1)
    rowi = lax.broadcasted_iota(I32, (LANES, LANES), 0)

    def attend(k, v, mask):
        s = jnp.where(mask, _dot(k, qb) * ATTN_SCALE, NEG_INF)
        e = jnp.exp(s - jnp.max(s, axis=0, keepdims=True))
        p = e / jnp.sum(e, axis=0, keepdims=True) * mask.astype(F32)
        return p, _dot_tn(p.astype(BF16), v)

    vis = ((rowi + 1) * CMP_BLOCK - 1 <= past_len) & (rowi < nc)
    pc, oc = attend(kc_ref[0], vc_ref[0], vis)

    a = pc + pltpu.roll(pc, LANES - 1, axis=1)
    a = a + pltpu.roll(a, LANES - 2, axis=1)
    pair = a + pltpu.roll(a, LANES - 1, axis=0)
    blk = rowi >> 1
    cand = ((rowi & 1) == 0) & (blk < ns)
    cur = past_len // SEL_BLOCK
    forced = (blk == 0) | (blk == cur) | (blk == cur - 1)
    val = jnp.where(blk * SEL_BLOCK <= past_len, jnp.where(forced, FORCE_SCORE, pair), -1.0)
    val = jnp.where(cand, val, -jnp.inf)
    sel = _top_k_mask(val, rowi, 0, SEL_TOPK, LANES)
    sel = jnp.where(((lane & (HPG - 1)) == 0) & (lane < N_HEADS), sel, 0.0)
    selh = sel
    for r in range(1, HPG):
        selh = selh + pltpu.roll(sel, r, axis=1)

    nk = n_pages * PAGE_SIZE
    nslab = 2 * N_KV
    tail = lax.broadcasted_iota(I32, (2 * SUBLANES, gw), 0) == 0
    for p in range(n_pages):
        for g in range(N_KV):
            cols = slice(g * HEAD_DIM, (g + 1) * HEAD_DIM)
            kbuf[p * PAGE_SIZE:(p + 1) * PAGE_SIZE, cols] = pages[p][pl.ds(g, PAGE_SIZE, stride=nslab), :].astype(BF16)
            vbuf[p * PAGE_SIZE:(p + 1) * PAGE_SIZE, cols] = pages[p][pl.ds(N_KV + g, PAGE_SIZE, stride=nslab), :].astype(BF16)
    kbuf[nk:nk + 2 * SUBLANES, :] = jnp.where(tail, snew_ref[0, :, 0:gw], 0.0).astype(BF16)
    vbuf[nk:nk + 2 * SUBLANES, :] = jnp.where(tail, snew_ref[0, :, gw:2 * gw], 0.0).astype(BF16)
    chosen = _dot(et_ref[...], selh.astype(BF16)) > 0.5
    _, osel = attend(kbuf[...], vbuf[...], chosen)

    nw = win_ref.shape[0] // nslab
    for g in range(N_KV):
        cols = slice(g * HEAD_DIM, (g + 1) * HEAD_DIM)
        kwbuf[0:nw, cols] = win_ref[pl.ds(g, nw, stride=nslab), :].astype(BF16)
        vwbuf[0:nw, cols] = win_ref[pl.ds(N_KV + g, nw, stride=nslab), :].astype(BF16)
    kwbuf[nw:nw + 2 * SUBLANES, :] = jnp.where(tail, wnew_ref[0, :, 0:gw], 0.0).astype(BF16)
    vwbuf[nw:nw + 2 * SUBLANES, :] = jnp.where(tail, wnew_ref[0, :, gw:2 * gw], 0.0).astype(BF16)
    wrow = lax.broadcasted_iota(I32, (nw + 2 * SUBLANES, LANES), 0)
    _, owin = attend(kwbuf[...], vwbuf[...], wrow <= nw)
    wout_ref[0:(nw - 1) * nslab, :] = win_ref[nslab:nw * nslab, :]
    wout_ref[(nw - 1) * nslab:nw * nslab, :] = wslab_ref[...]

    gates = jax.nn.sigmoid(ng_ref[0])
    for h in range(N_HEADS):
        g, hh = divmod(h, HPG)
        c = g * LANES + hh * 3
        cols = slice(g * HEAD_DIM, (g + 1) * HEAD_DIM)
        o_h = (gates[:, c:c + 1] * oc[h:h + 1, cols] + gates[:, c + 1:c + 2] * osel[h:h + 1, cols]
               + gates[:, c + 2:c + 3] * owin[h:h + 1, cols])
        o_ref[0, :, h * HEAD_DIM:(h + 1) * HEAD_DIM] = o_h.astype(o_ref.dtype)


def step_attention(qblk, ng, kc, vc, sel_pool, page_table, sel_new, win_state, win_new, expand_t, past_len):
    n, n_pages = page_table.shape
    gw = N_KV * HEAD_DIM
    nslab = 2 * N_KV
    nk = n_pages * PAGE_SIZE
    nw = win_state.shape[0] // (n * nslab)
    nc = (past_len + 1) // CMP_BLOCK
    ns = max(-(-(past_len + 1) // SEL_BLOCK), SEL_TOPK)
    assert nc <= LANES and 2 * ns <= LANES and nw == WINDOW and past_len >= WINDOW
    kern = functools.partial(_step_attn_kernel, n_pages=n_pages, past_len=past_len, nc=nc, ns=ns)
    page_specs = [pl.BlockSpec((PAGE_SIZE * nslab, HEAD_DIM), functools.partial(lambda i, pt, p: (pt[i, p], 0), p=p))
                  for p in range(n_pages)]
    grid_spec = pltpu.PrefetchScalarGridSpec(
        num_scalar_prefetch=1,
        grid=(n,),
        in_specs=[pl.BlockSpec((1, gw, LANES), lambda i, pt: (i, 0, 0)),
                  pl.BlockSpec((1, 1, N_KV * LANES), lambda i, pt: (i, 0, 0)),
                  pl.BlockSpec((1, LANES, gw), lambda i, pt: (i, 0, 0)),
                  pl.BlockSpec((1, LANES, gw), lambda i, pt: (i, 0, 0))]
        + page_specs
        + [pl.BlockSpec((1, 1, 2 * gw), lambda i, pt: (i, 0, 0)),
           pl.BlockSpec((nw * nslab, HEAD_DIM), lambda i, pt: (i, 0)),
           pl.BlockSpec((1, 1, 2 * gw), lambda i, pt: (i, 0, 0)),
           pl.BlockSpec((nslab, HEAD_DIM), lambda i, pt: (i, 0)),
           pl.BlockSpec((nk + 2 * SUBLANES, LANES), lambda i, pt: (0, 0))],
        out_specs=[pl.BlockSpec((1, 1, N_HEADS * HEAD_DIM), lambda i, pt: (i, 0, 0)),
                   pl.BlockSpec((nw * nslab, HEAD_DIM), lambda i, pt: (i, 0))],
        scratch_shapes=[pltpu.VMEM((nk + 2 * SUBLANES, gw), BF16), pltpu.VMEM((nk + 2 * SUBLANES, gw), BF16),
                        pltpu.VMEM((nw + 2 * SUBLANES, gw), BF16), pltpu.VMEM((nw + 2 * SUBLANES, gw), BF16)],
    )
    return pl.pallas_call(
        kern,
        grid_spec=grid_spec,
        out_shape=[jax.ShapeDtypeStruct((n, 1, N_HEADS * HEAD_DIM), BF16),
                   jax.ShapeDtypeStruct(win_state.shape, F32)],
        compiler_params=_params(("arbitrary",)),
        name="step_attn",
    )(page_table, qblk, ng, kc, vc, *([sel_pool] * n_pages), sel_new, win_state, win_new,
      win_new.reshape(n * nslab, HEAD_DIM), expand_t)


def _merge_kernel(x_ref, ya_ref, ob_ref, wga_ref, wgb_ref, wco_ref, wao_ref, o_ref):
    x = x_ref[...]
    ga = jax.nn.sigmoid(_dot(x, wga_ref[...]))
    gb = jax.nn.sigmoid(_dot(x, wgb_ref[...]))
    y_a = _dot(ya_ref[...], wco_ref[...])
    y_b = _dot(ob_ref[...], wao_ref[...])
    o_ref[...] = (ga * y_a + gb * y_b).astype(o_ref.dtype)


def merge_mixers(xb, ya_in, ob, w_merge, w_conv_out, w_attn_out, tm=512, tn=512):
    m, d = xb.shape
    tm = min(tm, m)
    assert m % tm == 0 and d % tn == 0
    nj = d // tn
    row = lambda j, i: (i, 0)
    col = lambda j, i: (0, j)
    return pl.pallas_call(
        _merge_kernel,
        grid=(nj, m // tm),
        in_specs=[pl.BlockSpec((tm, d), row), pl.BlockSpec((tm, ya_in.shape[1]), row),
                  pl.BlockSpec((tm, ob.shape[1]), row),
                  pl.BlockSpec((d, tn), col), pl.BlockSpec((d, tn), lambda j, i: (0, j + nj)),
                  pl.BlockSpec((w_conv_out.shape[0], tn), col), pl.BlockSpec((w_attn_out.shape[0], tn), col)],
        out_specs=pl.BlockSpec((tm, tn), lambda j, i: (i, j)),
        out_shape=jax.ShapeDtypeStruct((m, d), BF16),
        compiler_params=_params(("parallel", "parallel")),
        name="merge_mixers",
    )(xb, ya_in, ob, w_merge, w_merge, w_conv_out, w_attn_out)


def _layer_norm(z, g, b):
    mu = jnp.mean(z, axis=-1, keepdims=True)
    zc = z - mu
    var = jnp.mean(zc * zc, axis=-1, keepdims=True)
    return zc * lax.rsqrt(var + LN_EPS) * g + b


def _out_proj_ln_kernel(m_ref, w_ref, x_ref, g_ref, b_ref, *rest, alpha):
    o_ref = rest[-1]
    z = alpha * x_ref[...] + _dot(m_ref[...], w_ref[...])
    o_ref[...] = _layer_norm(z, g_ref[...], b_ref[...])


def out_proj_ln(mixin, w_o, x, g, b, alpha, total_rows, row_offset, prev=None, tm=512):
    m, d = x.shape
    tm = min(tm, m)
    assert m % tm == 0 and row_offset % tm == 0
    off = row_offset // tm
    row = lambda i: (i, 0)
    fixed = lambda i: (0, 0)
    in_specs = [pl.BlockSpec((tm, d), row), pl.BlockSpec((d, d), fixed), pl.BlockSpec((tm, d), row),
                pl.BlockSpec((1, d), fixed), pl.BlockSpec((1, d), fixed)]
    args = [mixin, w_o, x, g, b]
    aliases = {}
    if prev is not None:
        in_specs.append(pl.BlockSpec(memory_space=pl.ANY))
        args.append(prev)
        aliases = {5: 0}
    return pl.pallas_call(
        functools.partial(_out_proj_ln_kernel, alpha=alpha),
        grid=(m // tm,),
        in_specs=in_specs,
        out_specs=pl.BlockSpec((tm, d), lambda i: (i + off, 0)),
        out_shape=jax.ShapeDtypeStruct((total_rows, d), F32),
        input_output_aliases=aliases,
        compiler_params=_params(("parallel",)),
        name="out_proj_ln",
    )(*args)


def _route_kernel(h_ref, wr_ref, rb_ref, u_ref, idx_ref, gate_ref, rank_ref, cnt_ref, run_ref, *, tm):
    i = pl.program_id(0)

    @pl.when(i == 0)
    def _():
        run_ref[...] = jnp.zeros(run_ref.shape, F32)

    logits = lax.dot_general(wr_ref[...], h_ref[...], (((1,), (1,)), ((), ())),
                             precision=lax.Precision.HIGHEST, preferred_element_type=F32)
    s = jax.nn.sigmoid(logits)
    bsc = s + rb_ref[...]
    gsz = N_EXPERTS // N_GROUPS
    x3 = bsc.reshape(N_GROUPS, gsz, tm)
    sub = lax.broadcasted_iota(I32, (N_GROUPS, gsz, tm), 1).astype(F32)
    m1 = jnp.max(x3, axis=1, keepdims=True)
    i1 = jnp.min(jnp.where(x3 == m1, sub, float(gsz)), axis=1, keepdims=True)
    m2 = jnp.max(jnp.where(sub == i1, -jnp.inf, x3), axis=1, keepdims=True)
    gsc = (m1 + m2).reshape(N_GROUPS, tm)
    giota = lax.broadcasted_iota(I32, (N_GROUPS, tm), 0)
    gsel = _top_k_mask(gsc, giota, 0, TOPK_GROUPS, N_GROUPS)
    emask = jnp.broadcast_to(gsel.reshape(N_GROUPS, 1, tm), (N_GROUPS, gsz, tm)).reshape(N_EXPERTS, tm) > 0.5
    val = jnp.where(emask, bsc, NEG_INF)
    eiota = lax.broadcasted_iota(I32, (N_EXPERTS, tm), 0).astype(F32)
    hits, idxs, sels = [], [], []
    for _ in range(MOE_TOPK):
        m = jnp.max(val, axis=0, keepdims=True)
        first = jnp.min(jnp.where(val == m, eiota, float(N_EXPERTS)), axis=0, keepdims=True)
        hit = eiota == first
        hits.append(hit)
        idxs.append(first.astype(I32))
        sels.append(jnp.sum(jnp.where(hit, s, 0.0), axis=0, keepdims=True))
        val = jnp.where(hit, -jnp.inf, val)
    den = sels[0]
    for k in range(1, MOE_TOPK):
        den = den + sels[k]
    onehot = hits[0].astype(F32)
    for k in range(1, MOE_TOPK):
        onehot = onehot + hits[k].astype(F32)
    before = _dot(onehot.astype(BF16), u_ref[...]) + run_ref[:, 0:1]
    pad = SUBLANES - MOE_TOPK
    zi = jnp.zeros((pad, tm), I32)
    zf = jnp.zeros((pad, tm), F32)
    ranks = [jnp.sum(jnp.where(hits[k], before, 0.0), axis=0, keepdims=True).astype(I32) for k in range(MOE_TOPK)]
    idx_ref[...] = jnp.concatenate(idxs + [zi], axis=0)
    gate_ref[...] = jnp.concatenate([sels[k] / den * ROUTED_SCALE for k in range(MOE_TOPK)] + [zf], axis=0)
    rank_ref[...] = jnp.concatenate(ranks + [zi], axis=0)
    run_ref[...] = run_ref[...] + jnp.sum(onehot, axis=1, keepdims=True)
    cnt_ref[...] = run_ref[...].astype(I32)


def route(h, w_router_t, router_bias, tm=ROUTE_TILE):
    t, d = h.shape
    assert t % tm == 0
    upper = (lax.broadcasted_iota(I32, (tm, tm), 0) < lax.broadcasted_iota(I32, (tm, tm), 1)).astype(BF16)
    tok = lambda i: (0, i)
    fixed = lambda i: (0, 0)
    return pl.pallas_call(
        functools.partial(_route_kernel, tm=tm),
        grid=(t // tm,),
        in_specs=[pl.BlockSpec((tm, d), lambda i: (i, 0)), pl.BlockSpec((N_EXPERTS, d), fixed),
                  pl.BlockSpec((N_EXPERTS, 1), fixed), pl.BlockSpec((tm, tm), fixed)],
        out_specs=[pl.BlockSpec((SUBLANES, tm), tok), pl.BlockSpec((SUBLANES, tm), tok),
                   pl.BlockSpec((SUBLANES, tm), tok), pl.BlockSpec((N_EXPERTS, LANES), fixed)],
        out_shape=[jax.ShapeDtypeStruct((SUBLANES, t), I32), jax.ShapeDtypeStruct((SUBLANES, t), F32),
                   jax.ShapeDtypeStruct((SUBLANES, t), I32), jax.ShapeDtypeStruct((N_EXPERTS, LANES), I32)],
        scratch_shapes=[pltpu.VMEM((N_EXPERTS, LANES), F32)],
        compiler_params=_params(("arbitrary",)),
        name="moe_route",
    )(h, w_router_t, router_bias.reshape(N_EXPERTS, 1), upper)


def _row_copy(src, src_row, dst, dst_row, sem):
    return pltpu.make_async_copy(src.at[pl.ds(src_row, 1)], dst.at[pl.ds(dst_row, 1)], sem)


def _dispatch_kernel(pe_ref, dest_ref, h_ref, xs_out, zbuf, sem, zsem, *, tb):

    @pl.when(pl.program_id(0) == 0)
    def _():
        zbuf[...] = jnp.zeros(zbuf.shape, zbuf.dtype)

        def nonempty(e):
            return pe_ref[e] > jnp.where(e > 0, pe_ref[jnp.maximum(e - 1, 0)], 0)

        def zero_tile(e):
            first = pl.multiple_of(pe_ref[e] - EXPERT_TILE, EXPERT_TILE)
            return pltpu.make_async_copy(zbuf, xs_out.at[pl.ds(first, EXPERT_TILE)], zsem)

        def zstart(e, carry):
            @pl.when(nonempty(e))
            def _():
                zero_tile(e).start()
            return carry

        def zwait(e, carry):
            @pl.when(nonempty(e))
            def _():
                zero_tile(e).wait()
            return carry

        lax.fori_loop(0, N_EXPERTS, zstart, 0)
        lax.fori_loop(0, N_EXPERTS, zwait, 0)

    def start(r, carry):
        for k in range(MOE_TOPK):
            _row_copy(h_ref, r, xs_out, dest_ref[0, k, r], sem).start(priority=k % DMA_PRIORITIES)
        return carry

    lax.fori_loop(0, tb, start, 0, unroll=DMA_UNROLL)
    for k in range(MOE_TOPK):
        pltpu.make_async_copy(h_ref, xs_out.at[pl.ds(0, tb)], sem).wait()


def dispatch(h, dest_tiles, pad_end, n_rows, tb=SCATTER_TILE):
    t, d = h.shape
    assert t % tb == 0
    grid_spec = pltpu.PrefetchScalarGridSpec(
        num_scalar_prefetch=1,
        grid=(t // tb,),
        in_specs=[pl.BlockSpec((1, SUBLANES, tb), lambda i, pe: (i, 0, 0), memory_space=pltpu.SMEM),
                  pl.BlockSpec((tb, d), lambda i, pe: (i, 0))],
        out_specs=pl.BlockSpec(memory_space=pl.ANY),
        scratch_shapes=[pltpu.VMEM((EXPERT_TILE, d), F32), pltpu.SemaphoreType.DMA, pltpu.SemaphoreType.DMA],
    )
    return pl.pallas_call(
        functools.partial(_dispatch_kernel, tb=tb),
        grid_spec=grid_spec,
        out_shape=jax.ShapeDtypeStruct((n_rows, d), F32),
        compiler_params=pltpu.CompilerParams(dimension_semantics=("arbitrary",), has_side_effects=True),
        name="moe_dispatch",
    )(pad_end, dest_tiles, h)


def _col_chunks(n, width=2 * LANES):
    return [(c, min(c + width, n)) for c in range(0, n, width)]


def _expert_up_kernel(te_ref, nu_ref, x_ref, w1_ref, w3_ref, h_ref):
    i = pl.program_id(0)

    @pl.when(i < nu_ref[0])
    def _():
        x = x_ref[...].astype(BF16)
        for c0, c1 in _col_chunks(h_ref.shape[1]):
            a = _dot(x, w1_ref[0, :, c0:c1].astype(BF16))
            b = _dot(x, w3_ref[0, :, c0:c1].astype(BF16))
            h_ref[:, c0:c1] = ((a * jax.nn.sigmoid(a)) * b).astype(h_ref.dtype)

    @pl.when(i >= nu_ref[0])
    def _():
        h_ref[...] = jnp.zeros(h_ref.shape, h_ref.dtype)


def _expert_down_kernel(te_ref, nu_ref, h_ref, w2_ref, o_ref):
    i = pl.program_id(0)

    @pl.when(i < nu_ref[0])
    def _():
        h = h_ref[...]
        for c0, c1 in _col_chunks(o_ref.shape[1]):
            o_ref[:, c0:c1] = _dot(h, w2_ref[0, :, c0:c1].astype(BF16))

    @pl.when(i >= nu_ref[0])
    def _():
        o_ref[...] = jnp.zeros(o_ref.shape, o_ref.dtype)


def expert_mlp_f32(xs, tile_expert, n_used, w1, w3, w2, tm=EXPERT_TILE):
    r, d = xs.shape
    f = w1.shape[2]
    assert r % tm == 0
    nt = r // tm
    rows = lambda i, te, nu: (jnp.minimum(i, nu[0] - 1), 0)
    own = lambda i, te, nu: (i, 0)
    wsel = lambda i, te, nu: (te[jnp.minimum(i, nu[0] - 1)], 0, 0)
    big = pltpu.CompilerParams(dimension_semantics=("arbitrary",), vmem_limit_bytes=60 * 1024 * 1024)
    h = pl.pallas_call(
        _expert_up_kernel,
        grid_spec=pltpu.PrefetchScalarGridSpec(
            num_scalar_prefetch=2, grid=(nt,),
            in_specs=[pl.BlockSpec((tm, d), rows), pl.BlockSpec((1, d, f), wsel), pl.BlockSpec((1, d, f), wsel)],
            out_specs=pl.BlockSpec((tm, f), own)),
        out_shape=jax.ShapeDtypeStruct((r, f), BF16),
        compiler_params=big,
        name="expert_up",
    )(tile_expert, n_used, xs, w1, w3)
    return pl.pallas_call(
        _expert_down_kernel,
        grid_spec=pltpu.PrefetchScalarGridSpec(
            num_scalar_prefetch=2, grid=(nt,),
            in_specs=[pl.BlockSpec((tm, f), rows), pl.BlockSpec((1, f, d), wsel)],
            out_specs=pl.BlockSpec((tm, d), own)),
        out_shape=jax.ShapeDtypeStruct((r, d), F32),
        compiler_params=big,
        name="expert_down",
    )(tile_expert, n_used, h, w2)


def _combine_kernel(dest_ref, next_ref, gate_ref, h_ref, sh_ref, g_ref, b_ref, o_hbm, y_ref, buf, sems,
                    *, tb, alpha, n_steps):
    i = pl.program_id(0)

    def fetch(d_ref, s):
        def start(r, carry):
            for k in range(MOE_TOPK):
                _row_copy(o_hbm, d_ref[0, k, r], buf.at[s, k], r, sems.at[s]).start(priority=k % DMA_PRIORITIES)
            return carry

        lax.fori_loop(0, tb, start, 0, unroll=DMA_UNROLL)

    def finish(s):
        for k in range(MOE_TOPK):
            pltpu.make_async_copy(o_hbm.at[pl.ds(0, tb)], buf.at[s, k], sems.at[s]).wait()
        gate = gate_ref[...]
        routed = gate[:, 0:1] * buf[s, 0]
        for k in range(1, MOE_TOPK):
            routed = routed + gate[:, k:k + 1] * buf[s, k]
        z = alpha * h_ref[...] + (routed + sh_ref[...])
        y_ref[...] = _layer_norm(z, g_ref[...], b_ref[...])

    @pl.when(i == 0)
    def _():
        fetch(dest_ref, 0)

    for s in range(2):
        @pl.when((i + 1 < n_steps) & ((i + 1) % 2 == s))
        def _(s=s):
            fetch(next_ref, s)

    for s in range(2):
        @pl.when(i % 2 == s)
        def _(s=s):
            finish(s)


def combine_ln(dest_tiles, gate_t, h, shared, o_sorted, g, b, alpha, tb=COMBINE_TILE):
    t, d = h.shape
    assert t % tb == 0
    n_steps = t // tb
    row = lambda i: (i, 0)
    fixed = lambda i: (0, 0)
    return pl.pallas_call(
        functools.partial(_combine_kernel, tb=tb, alpha=alpha, n_steps=n_steps),
        grid=(n_steps,),
        in_specs=[pl.BlockSpec((1, SUBLANES, tb), lambda i: (i, 0, 0), memory_space=pltpu.SMEM),
                  pl.BlockSpec((1, SUBLANES, tb), lambda i: (jnp.minimum(i + 1, n_steps - 1), 0, 0),
                               memory_space=pltpu.SMEM),
                  pl.BlockSpec((tb, SUBLANES), row), pl.BlockSpec((tb, d), row), pl.BlockSpec((tb, d), row),
                  pl.BlockSpec((1, d), fixed), pl.BlockSpec((1, d), fixed),
                  pl.BlockSpec(memory_space=pl.ANY)],
        out_specs=pl.BlockSpec((tb, d), row),
        out_shape=jax.ShapeDtypeStruct((t, d), F32),
        scratch_shapes=[pltpu.VMEM((2, MOE_TOPK, tb, d), F32), pltpu.SemaphoreType.DMA((2,))],
        compiler_params=_params(("arbitrary",)),
        name="moe_combine",
    )(dest_tiles, dest_tiles, gate_t, h, shared, g, b, o_sorted)


def moe_ffn_ln(h1, w_router, router_bias, w1, w3, w2, ws1, ws3, ws2, g, b, alpha):
    t, d = h1.shape
    idx, gate, rank, cnt = route(h1, w_router.T, router_bias)
    counts = cnt[:, 0]
    padded = (counts + EXPERT_TILE - 1) // EXPERT_TILE * EXPERT_TILE
    e_ids = jnp.arange(N_EXPERTS, dtype=I32)
    pad_end = jnp.sum(jnp.where(e_ids[None, :] <= e_ids[:, None], padded[None, :], 0), axis=1)
    pad_start = pad_end - padded
    n_tiles = -(-(t * MOE_TOPK) // EXPERT_TILE) + N_EXPERTS
    tile_start = jnp.arange(n_tiles, dtype=I32) * EXPERT_TILE
    tile_expert = jnp.minimum(jnp.sum((pad_end[None, :] <= tile_start[:, None]).astype(I32), axis=1),
                              N_EXPERTS - 1)
    n_used = (pad_end[-1] // EXPERT_TILE).astype(I32).reshape(1)
    dest = jnp.sum(jnp.where(idx[:, :, None] == e_ids, pad_start, 0), axis=-1) + rank

    def tiles(a, tb):
        return a.reshape(SUBLANES, t // tb, tb).transpose(1, 0, 2)

    xs = dispatch(h1, tiles(dest, SCATTER_TILE), pad_end.astype(I32), n_tiles * EXPERT_TILE)
    o_sorted = expert_mlp_f32(xs, tile_expert, n_used, w1, w3, w2)
    n_dense = t // ROUTE_TILE
    shared = expert_mlp_f32(h1, jnp.zeros((n_dense,), I32), jnp.full((1,), n_dense, I32),
                            ws1[None], ws3[None], ws2[None], tm=ROUTE_TILE)
    return combine_ln(tiles(dest, COMBINE_TILE), gate.T, h1, shared, o_sorted, g, b, alpha)


def _split_w_in(w_in, d_conv):
    d = w_in.shape[0]
    hd = N_HEADS * HEAD_DIM
    kvw = 2 * N_KV * HEAD_DIM
    o = 0
    parts = {}
    for name, width in (("b", d_conv), ("c", d_conv), ("h", d_conv), ("q", hd), ("cmp", kvw), ("sel", kvw),
                        ("win", kvw), ("nsa", 3 * N_HEADS), ("merge", 2 * d)):
        parts[name] = w_in[:, o:o + width]
        o += width
    nsa = parts["nsa"].reshape(d, N_KV, HPG * 3)
    parts["nsa"] = jnp.pad(nsa, ((0, 0), (0, 0), (0, LANES - HPG * 3))).reshape(d, N_KV * LANES)
    return {k: v.astype(BF16) for k, v in parts.items()}


def kernel(x_prompt, x_sample, cache_cmp_kv, cache_sel_kv, state_win_kv, state_conv, page_table,
           w_in, conv_w, w_phi_k, w_phi_v, w_conv_out, w_attn_out, w_o, ln1_g, ln1_b,
           w_router, router_bias, w_e_gate, w_e_up, w_e_down, w_s_gate, w_s_up, w_s_down,
           ln2_g, ln2_b):
    depth = w_in.shape[0]
    assert depth == 1
    alpha = (2.0 * depth) ** 0.25
    bp, tp, d = x_prompt.shape
    ns_, ts, _ = x_sample.shape
    assert ts == 1
    n_pages = page_table.shape[1]
    past_len = n_pages * PAGE_SIZE
    d_conv = conv_w.shape[2]
    kvw = 2 * N_KV * HEAD_DIM
    gw = N_KV * HEAD_DIM
    hd = N_HEADS * HEAD_DIM
    t_all = bp * tp + ns_

    w = _split_w_in(w_in[0], d_conv)
    cw = conv_w[0]
    wk_phi = w_phi_k[0].reshape(CMP_BLOCK * HEAD_DIM, HEAD_DIM).astype(BF16)
    wv_phi = w_phi_v[0].reshape(CMP_BLOCK * HEAD_DIM, HEAD_DIM).astype(BF16)
    wco = w_conv_out[0].astype(BF16)
    wao = w_attn_out[0].astype(BF16)
    wo = w_o[0].astype(BF16)
    g1, b1 = ln1_g[0].reshape(1, d), ln1_b[0].reshape(1, d)
    g2, b2 = ln2_g[0].reshape(1, d), ln2_b[0].reshape(1, d)

    xpb = x_prompt.astype(BF16)
    xp2 = xpb.reshape(bp * tp, d)
    ya_p, st_p = conv_proj(xpb, w["b"], w["c"], w["h"], cw, jnp.zeros((bp, SUBLANES, d_conv), F32))
    (q_p,) = matmul(xp2, w["q"], (BF16,))
    (cmp_p,) = kv_proj(xp2, w["cmp"], False)
    sel_p, sel_pb = kv_proj(xp2, w["sel"], True)
    win_p, win_pb = kv_proj(xp2, w["win"], True)
    (ng_p,) = matmul(xp2, w["nsa"], (F32,))
    nc_p = tp // CMP_BLOCK
    slabs = lambda a: a.reshape(-1, HEAD_DIM)
    kc_p, vc_p = cmp_kv(cmp_p, wk_phi, wv_phi, nb=nc_p)
    pad_c = lambda a: jnp.pad(a, ((0, 0), (0, 0), (0, LANES - a.shape[2]), (0, 0))).astype(BF16)
    blk_of_key = jnp.arange(tp, dtype=I32) // SEL_BLOCK
    expand = (jnp.arange(LANES, dtype=I32)[:, None] == 2 * blk_of_key[None, :]).astype(BF16)
    o_p = prompt_attention(q_p.reshape(bp, tp, hd), ng_p.reshape(bp, tp, N_KV * LANES), pad_c(kc_p), pad_c(vc_p),
                           sel_pb.reshape(bp, tp, kvw), win_pb.reshape(bp, tp, kvw), expand)
    mix_p = merge_mixers(xp2, ya_p.reshape(bp * tp, d_conv), o_p.reshape(bp * tp, hd), w["merge"], wco, wao)
    h1 = out_proj_ln(mix_p, wo, x_prompt.reshape(bp * tp, d), g1, b1, alpha, t_all, 0,
                     prev=jnp.zeros((t_all, d), F32))

    xs2 = x_sample.reshape(ns_, d)
    xsb = xs2.astype(BF16)
    w_conv3 = jnp.concatenate([w["b"], w["c"], w["h"]], axis=1)
    (pconv_s,) = matmul(xsb, w_conv3, (F32,))
    ya_s, u_s = step_conv(pconv_s, state_conv[0, :, 0], state_conv[0, :, 1], cw)
    (q_s,) = matmul(xsb, w["q"], (BF16,))
    (cmp_s,) = matmul(xsb, w["cmp"], (F32,))
    (sel_s,) = matmul(xsb, w["sel"], (F32,))
    (win_s,) = matmul(xsb, w["win"], (F32,))
    (ng_s,) = matmul(xsb, w["nsa"], (F32,))
    n_phys = cache_cmp_kv.shape[1]
    bpp = PAGE_SIZE // CMP_BLOCK
    pages_per_tile = 32
    assert n_phys % pages_per_tile == 0
    kc_all, vc_all = cmp_kv(slabs(cache_cmp_kv), wk_phi, wv_phi, nb=pages_per_tile * bpp)

    def per_sequence(a):
        a = a.reshape(n_phys // pages_per_tile, N_KV, pages_per_tile, bpp, HEAD_DIM)
        a = a.transpose(0, 2, 3, 1, 4).reshape(n_phys, bpp, gw)
        a = a[page_table].reshape(ns_, n_pages * bpp, gw)
        return jnp.pad(a, ((0, 0), (0, LANES - n_pages * bpp), (0, 0))).astype(BF16)

    head_group = jnp.arange(LANES, dtype=I32) // HPG
    qh = jnp.pad(q_s.reshape(ns_, N_HEADS, HEAD_DIM), ((0, 0), (0, LANES - N_HEADS), (0, 0)))
    qblk = jnp.where(jnp.arange(N_KV, dtype=I32)[None, :, None, None] == head_group[None, None, None, :],
                     qh.transpose(0, 2, 1)[:, None], 0).reshape(ns_, gw, LANES).astype(BF16)
    nk = n_pages * PAGE_SIZE
    key_row = jnp.arange(nk + 2 * SUBLANES, dtype=I32)
    expand_t = ((2 * (key_row // SEL_BLOCK))[:, None] == jnp.arange(LANES, dtype=I32)[None, :]) & (key_row <= nk)[:, None]
    o_s, win_next = step_attention(qblk, ng_s.reshape(ns_, 1, N_KV * LANES), per_sequence(kc_all), per_sequence(vc_all),
                         slabs(cache_sel_kv), page_table, sel_s.reshape(ns_, 1, kvw), slabs(state_win_kv),
                         win_s.reshape(ns_, 1, kvw), expand_t.astype(BF16), past_len)
    mix_s = merge_mixers(xsb, ya_s, o_s.reshape(ns_, hd), w["merge"], wco, wao)
    h1 = out_proj_ln(mix_s, wo, xs2, g1, b1, alpha, t_all, bp * tp, prev=h1)

    y = moe_ffn_ln(h1, w_router[0], router_bias[0], w_e_gate[0], w_e_up[0], w_e_down[0],
                   w_s_gate[0], w_s_up[0], w_s_down[0], g2, b2, alpha)

    kv6 = lambda a, n, t: a.reshape(1, n, t, 2, N_KV, HEAD_DIM)
    win_keep = min(WINDOW, tp)
    new_win_s = win_next.reshape(state_win_kv.shape)
    new_conv_s = jnp.stack([state_conv[0, :, 1], u_s], axis=1)[None]
    return (y[:bp * tp].reshape(bp, tp, d), y[bp * tp:].reshape(ns_, ts, d),
            kv6(cmp_p, bp, tp), kv6(sel_p, bp, tp), kv6(win_p, bp, tp)[:, :, tp - win_keep:],
            st_p[None, :, SUBLANES - (CONV_WIDTH - 1):],
            kv6(cmp_s, ns_, 1), kv6(sel_s, ns_, 1), new_win_s, new_conv_s)
```
